```python
import math
import jax, jax.numpy as jnp
from jax import lax
import numpy as np

D_MODEL = 1024
BATCH = 16
SEQ = 2048
DEPTH = 1
DEC_BATCH = 32
DEC_SEQ = 4
PAST_LEN = 16384
PAGE_SIZE = 128

R_HEADS = 4
R_DK = 128
R_DV = 256
R_CHUNK = 128
D_HEADS = 8
D_DH = 64
D_DV = 128
Q_BLOCK = 128
ROPE_THETA = 10000.0
N_EXPERTS = 32
TOP_K = 4
D_FF = 1024
SWIGLU_ALPHA = 1.702
SWIGLU_LIMIT = 7.0
MOE_BLOCK = 256
DN_ALPHA = (2.0 * DEPTH) ** 0.25
DN_BETA = (8.0 * DEPTH) ** -0.25
NEG_INF = -1e30
IN_WIDTHS = (R_HEADS * R_DK, R_HEADS * R_DK, R_HEADS * R_DV, R_HEADS * R_DV,
             D_HEADS * 2 * D_DH, D_HEADS * 2 * D_DH, D_HEADS * D_DV, D_MODEL, D_MODEL)
IN_SPLITS = tuple(int(s) for s in np.cumsum(IN_WIDTHS)[:-1])

kernel_name = 'hybrid_retention_diffattn_moe_step'


def rope(x, pos):
    half = x.shape[-1] // 2
    inv = ROPE_THETA ** (-jnp.arange(half, dtype=jnp.float32) / half)
    ang = pos.astype(jnp.float32)[:, None] * inv[None, :]
    ang = ang.reshape(ang.shape[:1] + (1,) * (x.ndim - 3) + (half,))
    cos, sin = jnp.cos(ang), jnp.sin(ang)
    xf = x.astype(jnp.float32)
    x1, x2 = xf[..., :half], xf[..., half:]
    return jnp.concatenate([x1 * cos - x2 * sin, x1 * sin + x2 * cos], axis=-1).astype(x.dtype)


def layer_norm(x, g, b, eps=1e-5):
    xf = x.astype(jnp.float32)
    mu = jnp.mean(xf, axis=-1, keepdims=True)
    var = jnp.mean(jnp.square(xf - mu), axis=-1, keepdims=True)
    return ((xf - mu) * lax.rsqrt(var + eps) * g.astype(jnp.float32) + b.astype(jnp.float32)).astype(x.dtype)


def head_group_norm(x, eps=1e-6):
    xf = x.astype(jnp.float32)
    mu = jnp.mean(xf, axis=-1, keepdims=True)
    var = jnp.mean(jnp.square(xf - mu), axis=-1, keepdims=True)
    return ((xf - mu) * lax.rsqrt(var + eps)).astype(x.dtype)


def head_rms_norm(x, g, eps=1e-5):
    xf = x.astype(jnp.float32)
    ms = jnp.mean(jnp.square(xf), axis=-1, keepdims=True)
    return (xf * lax.rsqrt(ms + eps) * g.astype(jnp.float32)).astype(x.dtype)


def retention_chunk(s_prev, q, k, v, log_gamma):
    c = q.shape[1]
    idx = jnp.arange(c, dtype=jnp.float32)
    diff = idx[:, None] - idx[None, :]
    causal = diff >= 0
    decay = jnp.where(causal[None], jnp.exp(log_gamma[:, None, None] * jnp.where(causal, diff, 0.0)[None]), 0.0)
    scores = jnp.einsum('bnhd,bmhd->bhnm', q, k) * decay[None]
    inner = jnp.einsum('bhnm,bmhe->bnhe', scores, v)
    q_decay = jnp.exp(log_gamma[None, :] * (idx[:, None] + 1.0))
    cross = jnp.einsum('bnhd,bhde->bnhe', q, s_prev) * q_decay[None, :, :, None]
    k_decay = jnp.exp(log_gamma[None, :] * (c - 1.0 - idx[:, None]))
    s_new = jnp.exp(log_gamma * c)[None, :, None, None] * s_prev + jnp.einsum(
        'bmhd,bmhe->bhde', k * k_decay[None, :, :, None], v)
    return s_new, inner + cross


def retention_prompt(q, k, v, log_gamma):
    b, s, h, dk = q.shape
    nc = s // R_CHUNK

    def to_chunks(a):
        return jnp.moveaxis(a.astype(jnp.float32).reshape(b, nc, R_CHUNK, h, a.shape[-1]), 1, 0)

    s0 = jnp.zeros((b, h, dk, v.shape[-1]), jnp.float32)

    def step(state, qkv):
        qc, kc, vc = qkv
        return retention_chunk(state, qc, kc, vc, log_gamma)

    s_fin, o = lax.scan(step, s0, (to_chunks(q), to_chunks(k), to_chunks(v)))
    return s_fin, jnp.moveaxis(o, 0, 1).reshape(b, s, h, v.shape[-1]).astype(v.dtype)


def retention_sample(state, q, k, v, log_gamma):
    s_new, o = retention_chunk(state.astype(jnp.float32), q.astype(jnp.float32),
                               k.astype(jnp.float32), v.astype(jnp.float32), log_gamma)
    return s_new, o.astype(v.dtype)


def diff_attn_prompt(q, k, v, lam):
    b, s, h, _, dh = q.shape
    nb = s // Q_BLOCK
    qb = jnp.moveaxis(q.reshape(b, nb, Q_BLOCK, h, 2, dh), 1, 0)
    kpos = jnp.arange(s)
    scale = dh ** -0.5

    def one_block(args):
        qblk, i = args
        sc = jnp.einsum('bqhcd,bkhcd->bhcqk', qblk, k).astype(jnp.float32) * scale
        qpos = i * Q_BLOCK + jnp.arange(Q_BLOCK)
        mask = kpos[None, :] <= qpos[:, None]
        p = jax.nn.softmax(jnp.where(mask, sc, NEG_INF), axis=-1)
        w = p[:, :, 0] - lam * p[:, :, 1]
        return jnp.einsum('bhqk,bkhe->bqhe', w.astype(v.dtype), v)

    o = lax.map(one_block, (qb, jnp.arange(nb)))
    return jnp.moveaxis(o, 0, 1).reshape(b, s, h, v.shape[-1])


def diff_attn_sample(q, k_new, v_new, k_past, v_past, lam):
    t = q.shape[1]
    p_len = k_past.shape[1]
    scale = q.shape[-1] ** -0.5
    s_past = jnp.einsum('bqhcd,bkhcd->bhcqk', q, k_past).astype(jnp.float32) * scale
    s_new = jnp.einsum('bqhcd,bkhcd->bhcqk', q, k_new).astype(jnp.float32) * scale
    tri = jnp.arange(t)[None, :] <= jnp.arange(t)[:, None]
    s_new = jnp.where(tri, s_new, NEG_INF)
    p = jax.nn.softmax(jnp.concatenate([s_past, s_new], axis=-1), axis=-1)
    w = (p[:, :, 0] - lam * p[:, :, 1]).astype(v_new.dtype)
    return (jnp.einsum('bhqk,bkhe->bqhe', w[..., :p_len], v_past)
            + jnp.einsum('bhqk,bkhe->bqhe', w[..., p_len:], v_new))


def moe(x, w_router, b_router, w_gu, b_gu, w_dn, b_dn):
    shp = x.shape
    xt = x.reshape(-1, D_MODEL)
    n_tok = xt.shape[0]
    logits = jnp.einsum('td,de->te', xt, w_router).astype(jnp.float32) + b_router.astype(jnp.float32)
    top_val, top_idx = lax.top_k(logits, TOP_K)
    gate = jax.nn.softmax(top_val, axis=-1)
    n_assign = n_tok * TOP_K
    flat_e = top_idx.reshape(n_assign)
    flat_tok = jnp.arange(n_assign, dtype=jnp.int32) // TOP_K
    order = jnp.argsort(flat_e, stable=True)
    e_sorted = flat_e[order]
    counts = jnp.bincount(flat_e, length=N_EXPERTS)
    padded = (counts + MOE_BLOCK - 1) // MOE_BLOCK * MOE_BLOCK
    start = jnp.cumsum(counts) - counts
    end_pad = jnp.cumsum(padded)
    start_pad = end_pad - padded
    dest = start_pad[e_sorted] + jnp.arange(n_assign) - start[e_sorted]
    n_rows = (n_assign + MOE_BLOCK - 1) // MOE_BLOCK * MOE_BLOCK + N_EXPERTS * MOE_BLOCK
    n_blocks = n_rows // MOE_BLOCK
    row_tok = jnp.full((n_rows,), n_tok, jnp.int32).at[dest].set(flat_tok[order])
    row_gate = jnp.zeros((n_rows,), jnp.float32).at[dest].set(gate.reshape(n_assign)[order])
    blk_e = jnp.minimum(jnp.searchsorted(end_pad, jnp.arange(n_blocks) * MOE_BLOCK, side='right'), N_EXPERTS - 1)
    x_pad = jnp.concatenate([xt, jnp.zeros((1, D_MODEL), xt.dtype)], axis=0)
    xb = x_pad[row_tok].reshape(n_blocks, MOE_BLOCK, D_MODEL)

    def expert_block(args):
        xblk, e = args
        hgu = xblk @ w_gu[e] + b_gu[e]
        glu = jnp.minimum(hgu[:, :D_FF], SWIGLU_LIMIT)
        lin = jnp.clip(hgu[:, D_FF:], -SWIGLU_LIMIT, SWIGLU_LIMIT)
        act = glu * jax.nn.sigmoid(SWIGLU_ALPHA * glu) * (lin + 1.0)
        return act @ w_dn[e] + b_dn[e]

    yb = lax.map(expert_block, (xb, blk_e))
    y_rows = yb.reshape(n_rows, D_MODEL) * row_gate[:, None].astype(yb.dtype)
    y = jnp.zeros((n_tok + 1, D_MODEL), yb.dtype).at[row_tok].add(y_rows)[:n_tok]
    return y.reshape(shp)


def hybrid_layer(x, pos, retention_fn, attention_fn, lam_init, p):
    b, s, _ = x.shape
    h = x @ p['w_in']
    qr, kr, vr, gr, qd, kd, vd, ga, gb = jnp.split(h, IN_SPLITS, axis=-1)
    qr = rope(qr.reshape(b, s, R_HEADS, R_DK), pos)
    kr = rope(kr.reshape(b, s, R_HEADS, R_DK), pos) * (R_DK ** -0.5)
    vr = vr.reshape(b, s, R_HEADS, R_DV)
    s_ret, o_r = retention_fn(qr, kr, vr)
    o_r = head_group_norm(o_r).reshape(b, s, R_HEADS * R_DV) * jax.nn.silu(gr)
    qd = rope(qd.reshape(b, s, D_HEADS, 2, D_DH), pos)
    kd = rope(kd.reshape(b, s, D_HEADS, 2, D_DH), pos)
    vd = vd.reshape(b, s, D_HEADS, D_DV)
    lam = (jnp.exp(jnp.sum(p['lam_q1'].astype(jnp.float32) * p['lam_k1'].astype(jnp.float32)))
           - jnp.exp(jnp.sum(p['lam_q2'].astype(jnp.float32) * p['lam_k2'].astype(jnp.float32))) + lam_init)
    o_d = attention_fn(qd, kd, vd, lam)
    o_d = (head_rms_norm(o_d, p['subln_w']) * (1.0 - lam_init)).reshape(b, s, D_HEADS * D_DV)
    mix = jax.nn.sigmoid(ga) * (o_r @ p['w_branch_ret']) + jax.nn.sigmoid(gb) * (o_d @ p['w_branch_diff'])
    x = layer_norm(DN_ALPHA * x + mix @ p['w_out'], p['ln1_g'], p['ln1_b'])
    y = moe(x, p['w_router'], p['b_router'], p['w_gate_up'], p['b_gate_up'], p['w_down'], p['b_down'])
    x = layer_norm(DN_ALPHA * x + y, p['ln2_g'], p['ln2_b'])
    return x, kd, vd, s_ret


def setup_inputs(seed: int = 0) -> dict:
    key = jax.random.key(seed)
    ks = jax.random.split(key, 26)
    f32 = jnp.float32
    n_pages = PAST_LEN // PAGE_SIZE
    n_pool = (DEC_BATCH * n_pages * 5) // 4
    in_width = sum(IN_WIDTHS)

    def nrm(k, shape, scale):
        return scale * jax.random.normal(k, shape, f32)

    return {
        'x_prompt': nrm(ks[0], (BATCH, SEQ, D_MODEL), 1.0),
        'x_sample': nrm(ks[1], (DEC_BATCH, DEC_SEQ, D_MODEL), 1.0),
        'cache_k': nrm(ks[2], (DEPTH, n_pool, PAGE_SIZE, D_HEADS, 2, D_DH), 1.0),
        'cache_v': nrm(ks[3], (DEPTH, n_pool, PAGE_SIZE, D_HEADS, D_DV), 1.0),
        'state_ret': nrm(ks[4], (DEPTH, DEC_BATCH, R_HEADS, R_DK, R_DV), 0.5),
        'page_table': jax.random.permutation(ks[5], n_pool)[: DEC_BATCH * n_pages].reshape(DEC_BATCH, n_pages).astype(jnp.int32),
        'w_in': nrm(ks[6], (DEPTH, D_MODEL, in_width), D_MODEL ** -0.5),
        'w_branch_ret': nrm(ks[7], (DEPTH, R_HEADS * R_DV, D_MODEL), (R_HEADS * R_DV) ** -0.5),
        'w_branch_diff': nrm(ks[8], (DEPTH, D_HEADS * D_DV, D_MODEL), (D_HEADS * D_DV) ** -0.5),
        'w_out': nrm(ks[9], (DEPTH, D_MODEL, D_MODEL), DN_BETA * D_MODEL ** -0.5),
        'lam_q1': nrm(ks[10], (DEPTH, D_DH), 0.1),
        'lam_k1': nrm(ks[11], (DEPTH, D_DH), 0.1),
        'lam_q2': nrm(ks[12], (DEPTH, D_DH), 0.1),
        'lam_k2': nrm(ks[13], (DEPTH, D_DH), 0.1),
        'subln_w': 1.0 + nrm(ks[14], (DEPTH, D_DV), 0.02),
        'ln1_g': 1.0 + nrm(ks[15], (DEPTH, D_MODEL), 0.02),
        'ln1_b': nrm(ks[16], (DEPTH, D_MODEL), 0.02),
        'w_router': nrm(ks[17], (DEPTH, D_MODEL, N_EXPERTS), D_MODEL ** -0.5),
        'b_router': nrm(ks[18], (DEPTH, N_EXPERTS), 0.01),
        'w_gate_up': nrm(ks[19], (DEPTH, N_EXPERTS, D_MODEL, 2 * D_FF), D_MODEL ** -0.5),
        'b_gate_up': nrm(ks[20], (DEPTH, N_EXPERTS, 2 * D_FF), 0.01),
        'w_down': nrm(ks[21], (DEPTH, N_EXPERTS, D_FF, D_MODEL), DN_BETA * D_FF ** -0.5),
        'b_down': nrm(ks[22], (DEPTH, N_EXPERTS, D_MODEL), 0.01),
        'ln2_g': 1.0 + nrm(ks[23], (DEPTH, D_MODEL), 0.02),
        'ln2_b': nrm(ks[24], (DEPTH, D_MODEL), 0.02),
    }


def reference(x_prompt, x_sample, cache_k, cache_v, state_ret, page_table, w_in, w_branch_ret, w_branch_diff,
              w_out, lam_q1, lam_k1, lam_q2, lam_k2, subln_w, ln1_g, ln1_b, w_router, b_router, w_gate_up,
              b_gate_up, w_down, b_down, ln2_g, ln2_b):
    db = x_sample.shape[0]
    pos_prompt = jnp.arange(x_prompt.shape[1])
    pos_sample = PAST_LEN + jnp.arange(x_sample.shape[1])
    log_gamma = jnp.log(1.0 - 2.0 ** (-5.0 - jnp.arange(R_HEADS, dtype=jnp.float32)))
    xp, xs = x_prompt, x_sample
    kp_l, vp_l, sp_l, ks_l, vs_l, ss_l = [], [], [], [], [], []
    for l in range(DEPTH):
        lam_init = 0.8 - 0.6 * math.exp(-0.3 * l)
        p = {'w_in': w_in[l], 'w_branch_ret': w_branch_ret[l], 'w_branch_diff': w_branch_diff[l],
             'w_out': w_out[l], 'lam_q1': lam_q1[l], 'lam_k1': lam_k1[l], 'lam_q2': lam_q2[l],
             'lam_k2': lam_k2[l], 'subln_w': subln_w[l], 'ln1_g': ln1_g[l], 'ln1_b': ln1_b[l],
             'w_router': w_router[l], 'b_router': b_router[l], 'w_gate_up': w_gate_up[l],
             'b_gate_up': b_gate_up[l], 'w_down': w_down[l], 'b_down': b_down[l],
             'ln2_g': ln2_g[l], 'ln2_b': ln2_b[l]}
        xp, kp, vp, sp = hybrid_layer(
            xp, pos_prompt,
            lambda q, k, v: retention_prompt(q, k, v, log_gamma),
            diff_attn_prompt, lam_init, p)
        k_past = cache_k[l][page_table].reshape(db, PAST_LEN, D_HEADS, 2, D_DH)
        v_past = cache_v[l][page_table].reshape(db, PAST_LEN, D_HEADS, D_DV)
        st = state_ret[l]
        xs, kn, vn, sn = hybrid_layer(
            xs, pos_sample,
            lambda q, k, v, st=st: retention_sample(st, q, k, v, log_gamma),
            lambda q, k, v, lam, kpa=k_past, vpa=v_past: diff_attn_sample(q, k, v, kpa, vpa, lam),
            lam_init, p)
        kp_l.append(kp); vp_l.append(vp); sp_l.append(sp)
        ks_l.append(kn); vs_l.append(vn); ss_l.append(sn)
    k_prompt = jnp.stack(kp_l, axis=0)
    v_prompt = jnp.stack(vp_l, axis=0)
    ret_state_prompt = jnp.stack(sp_l, axis=0)
    k_sample = jnp.stack(ks_l, axis=0)
    v_sample = jnp.stack(vs_l, axis=0)
    ret_state_sample = jnp.stack(ss_l, axis=0)
    return (xp, xs, k_prompt, v_prompt, ret_state_prompt, k_sample, v_sample, ret_state_sample)
```

```python
import functools
import math

import jax
import jax.numpy as jnp
from jax import lax
from jax.experimental import pallas as pl
from jax.experimental.pallas import tpu as pltpu

F32 = jnp.float32
BF16 = jnp.bfloat16

D_MODEL = 1024
PAST_LEN = 16384
PAGE_SIZE = 128
R_HEADS, R_DK, R_DV, R_CHUNK = 4, 128, 256, 128
D_HEADS, D_DH, D_DV = 8, 64, 128
ROPE_THETA = 10000.0
N_EXPERTS, TOP_K, D_FF = 32, 4, 1024
SWIGLU_ALPHA, SWIGLU_LIMIT = 1.702, 7.0
NEG_INF = -1e30

LANE = 128
VMEM_LIMIT = 56 * 1024 * 1024
ATTN_BLOCK = 512
DEC_PAGES = 8
MOE_TM = 512
TOK_BLOCK = 128
COL = 1024


def _cparams(*sem):
    return pltpu.CompilerParams(dimension_semantics=sem, vmem_limit_bytes=VMEM_LIMIT)


def _dot(a, b):
    return jnp.dot(a, b, preferred_element_type=F32)


def _dot_nt(a, b):
    return lax.dot_general(a, b, (((1,), (1,)), ((), ())), preferred_element_type=F32)


def _swap_halves(h, unit):
    outs = []
    lane = lax.broadcasted_iota(jnp.int32, (h.shape[0], LANE), 1)
    for g in range(h.shape[1] // LANE):
        sl = h[:, g * LANE:(g + 1) * LANE]
        if unit == LANE:
            outs.append(pltpu.roll(sl, LANE // 2, 1))
        else:
            fwd = pltpu.roll(sl, unit // 2, 1)
            bwd = pltpu.roll(sl, LANE - unit // 2, 1)
            outs.append(jnp.where((lane % unit) < unit // 2, bwd, fwd))
    return jnp.concatenate(outs, axis=1)


def _proj_kernel(x_ref, w_ref, o_ref):
    o_ref[...] = _dot(x_ref[...].astype(BF16), w_ref[...])


def _proj_rope_kernel(x_ref, w_ref, cos_ref, sin_ref, o_ref, *, unit):
    h = _dot(x_ref[...].astype(BF16), w_ref[...])
    o_ref[...] = h * cos_ref[...] + _swap_halves(h, unit) * sin_ref[...]


def _proj(x, w, col0, ncols, tm, rope=None):
    t = x.shape[0]
    grid = (t // tm, ncols // COL)
    in_specs = [pl.BlockSpec((tm, D_MODEL), lambda i, j: (i, 0)),
                pl.BlockSpec((D_MODEL, COL), lambda i, j: (0, col0 // COL + j))]
    args = [x, w]
    if rope is None:
        body = _proj_kernel
    else:
        cos, sin, unit = rope
        n_pos = cos.shape[0] // tm
        in_specs += [pl.BlockSpec((tm, COL), lambda i, j: (i % n_pos, j))] * 2
        args += [cos, sin]
        body = functools.partial(_proj_rope_kernel, unit=unit)
    return pl.pallas_call(
        body, grid=grid, in_specs=in_specs,
        out_specs=pl.BlockSpec((tm, COL), lambda i, j: (i, j)),
        out_shape=jax.ShapeDtypeStruct((t, ncols), F32),
        compiler_params=_cparams("parallel", "arbitrary"),
    )(*args)


def _rope_table(pos, unit, n_units, scale=1.0):
    half = unit // 2
    inv = ROPE_THETA ** (-jnp.arange(half, dtype=F32) / half)
    ang = pos.astype(F32)[:, None] * inv[None, :]
    cos, sin = jnp.cos(ang), jnp.sin(ang)
    c = jnp.concatenate([cos, cos], axis=-1) * scale
    s = jnp.concatenate([-sin, sin], axis=-1) * scale
    return jnp.tile(c, (1, n_units)), jnp.tile(s, (1, n_units))


def _ret_tables(chunk, n_tok):
    log_gamma = jnp.log(1.0 - 2.0 ** (-5.0 - jnp.arange(R_HEADS, dtype=F32)))
    idx = jnp.arange(chunk, dtype=F32)
    diff = idx[:, None] - idx[None, :]
    causal = diff >= 0
    dec = jnp.where(causal[None], jnp.exp(log_gamma[:, None, None] * jnp.where(causal, diff, 0.0)[None]), 0.0)
    qdec = jnp.exp(log_gamma[:, None] * (idx[None, :] + 1.0))
    kdec = jnp.exp(log_gamma[:, None] * (n_tok - 1.0 - idx[None, :]))
    sdec = jnp.exp(log_gamma * n_tok)
    return (dec,
            jnp.broadcast_to(qdec[:, :, None], (R_HEADS, chunk, R_DV)),
            jnp.broadcast_to(kdec[:, :, None], (R_HEADS, chunk, R_DK)),
            jnp.broadcast_to(sdec[:, None, None], (R_HEADS, 1, R_DV)))


def _ret_chunk(q, k, v, g, state, dec, qdec, kdec, sdec):
    qb, kb, vb = q.astype(BF16), k.astype(BF16), v.astype(BF16)
    scores = _dot_nt(qb, kb) * dec
    inner = _dot(scores.astype(BF16), vb)
    cross = _dot(qb, state.astype(BF16)) * qdec
    o = inner + cross
    kd_t = (k * kdec).T.astype(BF16)
    s_new = state * sdec + _dot(kd_t, vb)
    mu = jnp.mean(o, axis=-1, keepdims=True)
    var = jnp.mean(jnp.square(o - mu), axis=-1, keepdims=True)
    o = (o - mu) * lax.rsqrt(var + 1e-6)
    return o * (g * jax.nn.sigmoid(g)), s_new


def _ret_prompt_kernel(q_ref, k_ref, v_ref, g_ref, dec_ref, qdec_ref, kdec_ref, sdec_ref, o_ref, s_ref):
    @pl.when(pl.program_id(2) == 0)
    def _():
        s_ref[...] = jnp.zeros_like(s_ref)

    o, s_new = _ret_chunk(q_ref[...], k_ref[...], v_ref[...], g_ref[...], s_ref[0, 0],
                          dec_ref[0], qdec_ref[0], kdec_ref[0], sdec_ref[0])
    o_ref[...] = o
    s_ref[0, 0] = s_new


def _retention_prompt(qk, vr, gr, batch, seq):
    nc = seq // R_CHUNK
    tabs = _ret_tables(R_CHUNK, R_CHUNK)
    row = lambda b, h, c: b * nc + c
    return pl.pallas_call(
        _ret_prompt_kernel, grid=(batch, R_HEADS, nc),
        in_specs=[
            pl.BlockSpec((R_CHUNK, R_DK), lambda b, h, c: (row(b, h, c), h)),
            pl.BlockSpec((R_CHUNK, R_DK), lambda b, h, c: (row(b, h, c), R_HEADS + h)),
            pl.BlockSpec((R_CHUNK, R_DV), lambda b, h, c: (row(b, h, c), h)),
            pl.BlockSpec((R_CHUNK, R_DV), lambda b, h, c: (row(b, h, c), h)),
            pl.BlockSpec((1, R_CHUNK, R_CHUNK), lambda b, h, c: (h, 0, 0)),
            pl.BlockSpec((1, R_CHUNK, R_DV), lambda b, h, c: (h, 0, 0)),
            pl.BlockSpec((1, R_CHUNK, R_DK), lambda b, h, c: (h, 0, 0)),
            pl.BlockSpec((1, 1, R_DV), lambda b, h, c: (h, 0, 0)),
        ],
        out_specs=[
            pl.BlockSpec((R_CHUNK, R_DV), lambda b, h, c: (row(b, h, c), h)),
            pl.BlockSpec((1, 1, R_DK, R_DV), lambda b, h, c: (b, h, 0, 0)),
        ],
        out_shape=[jax.ShapeDtypeStruct((batch * seq, R_HEADS * R_DV), F32),
                   jax.ShapeDtypeStruct((batch, R_HEADS, R_DK, R_DV), F32)],
        compiler_params=_cparams("parallel", "parallel", "arbitrary"),
    )(qk, qk, vr, gr, *tabs)


def _ret_sample_kernel(q_ref, k_ref, v_ref, g_ref, s0_ref, dec_ref, qdec_ref, kdec_ref, sdec_ref,
                       o_ref, s_ref, qp, kp, vp, gp):
    n = q_ref.shape[1]
    for pad, src in ((qp, q_ref), (kp, k_ref), (vp, v_ref), (gp, g_ref)):
        pad[...] = jnp.zeros_like(pad)
        pad[0:n, :] = src[0]
    o, s_new = _ret_chunk(qp[...], kp[...], vp[...], gp[...], s0_ref[0, 0],
                          dec_ref[0], qdec_ref[0], kdec_ref[0], sdec_ref[0])
    o_ref[0] = o[0:n, :]
    s_ref[0, 0] = s_new


def _retention_sample(qk, vr, gr, state, n_tok):
    batch, rows, _ = qk.shape
    tabs = _ret_tables(R_CHUNK, n_tok)
    return pl.pallas_call(
        _ret_sample_kernel, grid=(batch, R_HEADS),
        in_specs=[
            pl.BlockSpec((1, rows, R_DK), lambda b, h: (b, 0, h)),
            pl.BlockSpec((1, rows, R_DK), lambda b, h: (b, 0, R_HEADS + h)),
            pl.BlockSpec((1, rows, R_DV), lambda b, h: (b, 0, h)),
            pl.BlockSpec((1, rows, R_DV), lambda b, h: (b, 0, h)),
            pl.BlockSpec((1, 1, R_DK, R_DV), lambda b, h: (b, h, 0, 0)),
            pl.BlockSpec((1, R_CHUNK, R_CHUNK), lambda b, h: (h, 0, 0)),
            pl.BlockSpec((1, R_CHUNK, R_DV), lambda b, h: (h, 0, 0)),
            pl.BlockSpec((1, R_CHUNK, R_DK), lambda b, h: (h, 0, 0)),
            pl.BlockSpec((1, 1, R_DV), lambda b, h: (h, 0, 0)),
        ],
        out_specs=[
            pl.BlockSpec((1, rows, R_DV), lambda b, h: (b, 0, h)),
            pl.BlockSpec((1, 1, R_DK, R_DV), lambda b, h: (b, h, 0, 0)),
        ],
        out_shape=[jax.ShapeDtypeStruct((batch, rows, R_HEADS * R_DV), F32),
                   jax.ShapeDtypeStruct((batch, R_HEADS, R_DK, R_DV), F32)],
        scratch_shapes=[pltpu.VMEM((R_CHUNK, R_DK), F32), pltpu.VMEM((R_CHUNK, R_DK), F32),
                        pltpu.VMEM((R_CHUNK, R_DV), F32), pltpu.VMEM((R_CHUNK, R_DV), F32)],
        compiler_params=_cparams("parallel", "parallel"),
    )(qk, qk, vr, gr, state, *tabs)


def _lambda(lq1_ref, lk1_ref, lq2_ref, lk2_ref, lam_init):
    a = jnp.sum(lq1_ref[...] * lk1_ref[...], axis=-1, keepdims=True)
    b = jnp.sum(lq2_ref[...] * lk2_ref[...], axis=-1, keepdims=True)
    return jnp.exp(a) - jnp.exp(b) + lam_init


def _sub_rms(o, w, lam_init):
    ms = jnp.mean(jnp.square(o), axis=-1, keepdims=True)
    return o * lax.rsqrt(ms + 1e-5) * w * (1.0 - lam_init)


def _dattn_prompt_kernel(lq1_ref, lk1_ref, lq2_ref, lk2_ref, w_ref, q_ref, k_ref, v_ref, o_ref,
                         kc_ref, vx_ref, *, blk, lam_init):
    qi = pl.program_id(2)

    @pl.when(qi == 0)
    def _():
        kc_ref[0] = k_ref[:, 0:D_DH].astype(BF16)
        kc_ref[1] = k_ref[:, D_DH:2 * D_DH].astype(BF16)
        vx_ref[:, 0:D_DV] = v_ref[...].astype(BF16)
        vx_ref[:, D_DV:2 * D_DV] = jnp.ones((vx_ref.shape[0], D_DV), BF16)

    q = q_ref[...] * (D_DH ** -0.5)
    row = lax.broadcasted_iota(jnp.int32, (blk, blk), 0)
    col = lax.broadcasted_iota(jnp.int32, (blk, blk), 1)
    outs = []
    for c in range(2):
        qc = q[:, c * D_DH:(c + 1) * D_DH].astype(BF16)

        def step(j, carry, masked, qc=qc, c=c):
            m, acc = carry
            start = pl.multiple_of(j * blk, blk)
            s = _dot_nt(qc, kc_ref[c, pl.ds(start, blk), :])
            if masked:
                s = jnp.where(col <= row, s, NEG_INF)
            m_new = jnp.maximum(m, jnp.max(s, axis=-1, keepdims=True))
            p = jnp.exp(s - m_new)
            acc = acc * jnp.exp(m - m_new) + _dot(p.astype(BF16), vx_ref[pl.ds(start, blk), :])
            return m_new, acc

        carry = (jnp.full((blk, 1), NEG_INF, F32), jnp.zeros((blk, 2 * D_DV), F32))
        carry = lax.fori_loop(0, qi, functools.partial(step, masked=False), carry)
        _, acc = step(qi, carry, True)
        outs.append(acc[:, 0:D_DV] / acc[:, D_DV:D_DV + 1])
    lam = _lambda(lq1_ref, lk1_ref, lq2_ref, lk2_ref, lam_init)
    o_ref[...] = _sub_rms(outs[0] - lam * outs[1], w_ref[...], lam_init)


def _lam_specs(n):
    zero = lambda *_: (0, 0)
    return [pl.BlockSpec((1, D_DH), zero)] * 4 + [pl.BlockSpec((1, D_DV), zero)]


def _dattn_prompt(lam_args, qd, kd, vd, batch, seq, lam_init):
    blk = min(ATTN_BLOCK, seq)
    nq = seq // blk
    return pl.pallas_call(
        functools.partial(_dattn_prompt_kernel, blk=blk, lam_init=lam_init),
        grid=(batch, D_HEADS, nq),
        in_specs=_lam_specs(3) + [
            pl.BlockSpec((blk, 2 * D_DH), lambda b, h, i: (b * nq + i, h)),
            pl.BlockSpec((seq, 2 * D_DH), lambda b, h, i: (b, h)),
            pl.BlockSpec((seq, D_DV), lambda b, h, i: (b, h)),
        ],
        out_specs=pl.BlockSpec((blk, D_DV), lambda b, h, i: (b * nq + i, h)),
        out_shape=jax.ShapeDtypeStruct((batch * seq, D_HEADS * D_DV), F32),
        scratch_shapes=[pltpu.VMEM((2, seq, D_DH), BF16), pltpu.VMEM((seq, 2 * D_DV), BF16)],
        compiler_params=_cparams("parallel", "parallel", "arbitrary"),
    )(*lam_args, qd, kd, vd)


DEC_ROWS = D_HEADS * 2 * 8


def _dattn_decode_kernel(pt_ref, lq1_ref, lk1_ref, lq2_ref, lk2_ref, w_ref, q_ref, kn_ref, vn_ref, *rest,
                         n_pages, lam_init):
    k_refs, v_refs = rest[:n_pages], rest[n_pages:2 * n_pages]
    o_ref, qbd_ref, m_ref, l_ref, acc_ref = rest[2 * n_pages:]
    j = pl.program_id(1)

    @pl.when(j == 0)
    def _():
        q8 = q_ref[0] * (D_DH ** -0.5)
        tiled = jnp.concatenate([q8] * (DEC_ROWS // 8), axis=0)
        r = lax.broadcasted_iota(jnp.int32, tiled.shape, 0)
        cidx = lax.broadcasted_iota(jnp.int32, tiled.shape, 1)
        qbd_ref[...] = jnp.where(r // 8 == cidx // D_DH, tiled, 0.0).astype(BF16)
        m_ref[...] = jnp.full(m_ref.shape, NEG_INF, F32)
        l_ref[...] = jnp.zeros(l_ref.shape, F32)
        acc_ref[...] = jnp.zeros(acc_ref.shape, F32)

    def update(s_list, v_list):
        m = m_ref[...]
        m_new = m
        for s in s_list:
            m_new = jnp.maximum(m_new, jnp.max(s, axis=-1, keepdims=True))
        alpha = jnp.exp(m - m_new)
        l = l_ref[...] * alpha
        acc = acc_ref[...] * alpha
        for s, v in zip(s_list, v_list):
            p = jnp.exp(s - m_new)
            l = l + jnp.sum(p, axis=-1, keepdims=True)
            acc = acc + _dot(p.astype(BF16), v)
        m_ref[...] = m_new
        l_ref[...] = l
        acc_ref[...] = acc

    qbd = qbd_ref[...]
    update([_dot_nt(qbd, k[0].astype(BF16)) for k in k_refs], [v[0].astype(BF16) for v in v_refs])

    @pl.when(j == pl.num_programs(1) - 1)
    def _():
        s = _dot_nt(qbd, kn_ref[0].astype(BF16))
        t_q = lax.broadcasted_iota(jnp.int32, s.shape, 0) % 8
        t_k = lax.broadcasted_iota(jnp.int32, s.shape, 1)
        update([jnp.where(t_k <= t_q, s, NEG_INF)], [vn_ref[0].astype(BF16)])
        acc = acc_ref[...] / l_ref[...]
        d = jnp.concatenate([acc[h * 16:(h + 1) * 16, h * D_DV:(h + 1) * D_DV] for h in range(D_HEADS)], axis=1)
        lam = _lambda(lq1_ref, lk1_ref, lq2_ref, lk2_ref, lam_init)
        o = d[0:8, :] - lam * d[8:16, :]
        w = w_ref[...]
        o_ref[0] = jnp.concatenate(
            [_sub_rms(o[:, h * D_DV:(h + 1) * D_DV], w, lam_init) for h in range(D_HEADS)], axis=1)


def _dattn_decode(lam_args, page_table, q8, kn, vn, cache_k, cache_v, lam_init):
    batch, n_pages = page_table.shape
    steps = n_pages // DEC_PAGES
    width = cache_k.shape[-1]

    def page_spec(p):
        return pl.BlockSpec((1, PAGE_SIZE, width), lambda b, j, pt: (pt[b * n_pages + j * DEC_PAGES + p], 0, 0))

    zero = lambda b, j, pt: (0, 0)
    seq_spec = lambda rows: pl.BlockSpec((1, rows, width), lambda b, j, pt: (b, 0, 0))
    grid_spec = pltpu.PrefetchScalarGridSpec(
        num_scalar_prefetch=1, grid=(batch, steps),
        in_specs=([pl.BlockSpec((1, D_DH), zero)] * 4 + [pl.BlockSpec((1, D_DV), zero)]
                  + [seq_spec(8), seq_spec(PAGE_SIZE), seq_spec(PAGE_SIZE)]
                  + [page_spec(p) for p in range(DEC_PAGES)] * 2),
        out_specs=seq_spec(8),
        scratch_shapes=[pltpu.VMEM((DEC_ROWS, width), BF16), pltpu.VMEM((DEC_ROWS, 1), F32),
                        pltpu.VMEM((DEC_ROWS, 1), F32), pltpu.VMEM((DEC_ROWS, width), F32)])
    return pl.pallas_call(
        functools.partial(_dattn_decode_kernel, n_pages=DEC_PAGES, lam_init=lam_init),
        grid_spec=grid_spec,
        out_shape=jax.ShapeDtypeStruct((batch, 8, width), F32),
        compiler_params=_cparams("parallel", "arbitrary"),
    )(page_table.reshape(-1), *lam_args, q8, kn, vn, *([cache_k] * DEC_PAGES), *([cache_v] * DEC_PAGES))


def _layer_norm(y, g, b):
    mu = jnp.mean(y, axis=-1, keepdims=True)
    var = jnp.mean(jnp.square(y - mu), axis=-1, keepdims=True)
    return (y - mu) * lax.rsqrt(var + 1e-5) * g + b


def _mix_kernel(x_ref, or_ref, od_ref, ga_ref, gb_ref, wr_ref, wd_ref, wo_ref, g_ref, b_ref, wrt_ref, brt_ref,
                x1_ref, idx_ref, gate_ref, *, dn_alpha):
    r = _dot(or_ref[...].astype(BF16), wr_ref[...])
    d = _dot(od_ref[...].astype(BF16), wd_ref[...])
    mix = jax.nn.sigmoid(ga_ref[...]) * r + jax.nn.sigmoid(gb_ref[...]) * d
    y = dn_alpha * x_ref[...] + _dot(mix.astype(BF16), wo_ref[...])
    x1 = _layer_norm(y, g_ref[...], b_ref[...])
    x1_ref[...] = x1
    vals = _dot(x1.astype(BF16), wrt_ref[...]) + brt_ref[...]
    col = lax.broadcasted_iota(jnp.int32, vals.shape, 1)
    lane = lax.broadcasted_iota(jnp.int32, idx_ref.shape, 1)
    idx_out = jnp.zeros(idx_ref.shape, jnp.int32)
    val_out = jnp.full(gate_ref.shape, NEG_INF, F32)
    for k in range(TOP_K):
        mx = jnp.max(vals, axis=-1, keepdims=True)
        first = jnp.min(jnp.where(vals == mx, col, N_EXPERTS), axis=-1, keepdims=True)
        idx_out = jnp.where(lane == k, first, idx_out)
        val_out = jnp.where(lane == k, mx, val_out)
        vals = jnp.where(col == first, NEG_INF, vals)
    e = jnp.exp(val_out - jnp.max(val_out, axis=-1, keepdims=True))
    idx_ref[...] = idx_out
    gate_ref[...] = e / jnp.sum(e, axis=-1, keepdims=True)


def _mix(x, o_r, o_d, ga, gb, wr, wd, wo, g, b, w_rt, b_rt, tm, dn_alpha):
    t = x.shape[0]
    tok = pl.BlockSpec((tm, D_MODEL), lambda i: (i, 0))
    mat = pl.BlockSpec((D_MODEL, D_MODEL), lambda i: (0, 0))
    vec = pl.BlockSpec((1, D_MODEL), lambda i: (0, 0))
    return pl.pallas_call(
        functools.partial(_mix_kernel, dn_alpha=dn_alpha), grid=(t // tm,),
        in_specs=[tok] * 5 + [mat] * 3 + [vec, vec,
                                          pl.BlockSpec((D_MODEL, N_EXPERTS), lambda i: (0, 0)),
                                          pl.BlockSpec((1, N_EXPERTS), lambda i: (0, 0))],
        out_specs=[tok, pl.BlockSpec((tm, LANE), lambda i: (i, 0)), pl.BlockSpec((tm, LANE), lambda i: (i, 0))],
        out_shape=[jax.ShapeDtypeStruct((t, D_MODEL), F32), jax.ShapeDtypeStruct((t, LANE), jnp.int32),
                   jax.ShapeDtypeStruct((t, LANE), F32)],
        compiler_params=_cparams("parallel"),
    )(x, o_r, o_d, ga, gb, wr, wd, wo, g, b, w_rt, b_rt)


def _scatter_rows_kernel(dest_ref, x_ref, xs_in_ref, xs_ref, sem):
    del xs_in_ref
    n = x_ref.shape[0]

    def copy(t, k):
        return pltpu.make_async_copy(x_ref.at[pl.ds(t, 1)], xs_ref.at[pl.ds(dest_ref[0, 0, t * TOP_K + k], 1)], sem)

    def start(t, c):
        for k in range(TOP_K):
            copy(t, k).start()
        return c

    def wait(t, c):
        for k in range(TOP_K):
            copy(t, k).wait()
        return c

    lax.fori_loop(0, n, start, 0)
    lax.fori_loop(0, n, wait, 0)


def _scatter_rows(dest3, x, xs):
    t = x.shape[0]
    return pl.pallas_call(
        _scatter_rows_kernel, grid=(t // TOK_BLOCK,),
        in_specs=[pl.BlockSpec((1, 1, TOK_BLOCK * TOP_K), lambda i: (i, 0, 0), memory_space=pltpu.SMEM),
                  pl.BlockSpec((TOK_BLOCK, D_MODEL), lambda i: (i, 0)),
                  pl.BlockSpec(memory_space=pl.ANY)],
        out_specs=pl.BlockSpec(memory_space=pl.ANY),
        out_shape=jax.ShapeDtypeStruct(xs.shape, xs.dtype),
        scratch_shapes=[pltpu.SemaphoreType.DMA],
        input_output_aliases={2: 0},
        compiler_params=_cparams("arbitrary"),
    )(dest3, x, xs)


def _expert_kernel(blk_e_ref, n_used_ref, x_ref, wgu_ref, bgu_ref, wdn_ref, bdn_ref, y_ref):
    del blk_e_ref
    used = pl.program_id(0) < n_used_ref[0]

    @pl.when(jnp.logical_not(used))
    def _():
        y_ref[...] = jnp.zeros_like(y_ref)

    @pl.when(used)
    def _():
        hgu = _dot(x_ref[...].astype(BF16), wgu_ref[0]) + bgu_ref[0]
        glu = jnp.minimum(hgu[:, 0:D_FF], SWIGLU_LIMIT)
        lin = jnp.clip(hgu[:, D_FF:2 * D_FF], -SWIGLU_LIMIT, SWIGLU_LIMIT)
        act = glu * jax.nn.sigmoid(SWIGLU_ALPHA * glu) * (lin + 1.0)
        y_ref[...] = _dot(act.astype(BF16), wdn_ref[0]) + bdn_ref[0]


def _experts(blk_e, n_used, xs, w_gu, b_gu, w_dn, b_dn):
    n_rows = xs.shape[0]
    blk = lambda i, be, nu: jnp.minimum(i, nu[0] - 1)
    exp = lambda i, be, nu: be[jnp.minimum(i, nu[0] - 1)]
    grid_spec = pltpu.PrefetchScalarGridSpec(
        num_scalar_prefetch=2, grid=(n_rows // MOE_TM,),
        in_specs=[pl.BlockSpec((MOE_TM, D_MODEL), lambda i, be, nu: (blk(i, be, nu), 0)),
                  pl.BlockSpec((1, D_MODEL, 2 * D_FF), lambda i, be, nu: (exp(i, be, nu), 0, 0)),
                  pl.BlockSpec((1, 1, 2 * D_FF), lambda i, be, nu: (exp(i, be, nu), 0, 0)),
                  pl.BlockSpec((1, D_FF, D_MODEL), lambda i, be, nu: (exp(i, be, nu), 0, 0)),
                  pl.BlockSpec((1, 1, D_MODEL), lambda i, be, nu: (exp(i, be, nu), 0, 0))],
        out_specs=pl.BlockSpec((MOE_TM, D_MODEL), lambda i, be, nu: (i, 0)))
    return pl.pallas_call(
        _expert_kernel, grid_spec=grid_spec,
        out_shape=jax.ShapeDtypeStruct((n_rows, D_MODEL), F32),
        compiler_params=_cparams("arbitrary"),
    )(blk_e, n_used, xs, w_gu, b_gu, w_dn, b_dn)


def _combine_kernel(dest_ref, gate_ref, x_ref, g_ref, b_ref, ys_ref, o_ref, buf, sem, *, dn_alpha):
    n = x_ref.shape[0]

    def copy(t, k):
        return pltpu.make_async_copy(ys_ref.at[pl.ds(dest_ref[0, 0, t * TOP_K + k], 1)], buf.at[k, pl.ds(t, 1)], sem)

    def start(t, c):
        for k in range(TOP_K):
            copy(t, k).start()
        return c

    def wait(t, c):
        for k in range(TOP_K):
            copy(t, k).wait()
        return c

    lax.fori_loop(0, n, start, 0)
    lax.fori_loop(0, n, wait, 0)
    gate = gate_ref[...]
    y = dn_alpha * x_ref[...]
    for k in range(TOP_K):
        y = y + gate[:, k:k + 1] * buf[k]
    o_ref[...] = _layer_norm(y, g_ref[...], b_ref[...])


def _combine(dest3, gates, x1, g, b, ys, dn_alpha):
    t = x1.shape[0]
    tok = pl.BlockSpec((TOK_BLOCK, D_MODEL), lambda i: (i, 0))
    vec = pl.BlockSpec((1, D_MODEL), lambda i: (0, 0))
    return pl.pallas_call(
        functools.partial(_combine_kernel, dn_alpha=dn_alpha), grid=(t // TOK_BLOCK,),
        in_specs=[pl.BlockSpec((1, 1, TOK_BLOCK * TOP_K), lambda i: (i, 0, 0), memory_space=pltpu.SMEM),
                  pl.BlockSpec((TOK_BLOCK, LANE), lambda i: (i, 0)), tok, vec, vec,
                  pl.BlockSpec(memory_space=pl.ANY)],
        out_specs=tok,
        out_shape=jax.ShapeDtypeStruct((t, D_MODEL), F32),
        scratch_shapes=[pltpu.VMEM((TOP_K, TOK_BLOCK, D_MODEL), F32), pltpu.SemaphoreType.DMA],
        compiler_params=_cparams("arbitrary"),
    )(dest3, gates, x1, g, b, ys)


def _route(top_idx):
    n_assign = top_idx.size
    flat_e = top_idx.reshape(n_assign)
    order = jnp.argsort(flat_e, stable=True)
    e_sorted = flat_e[order]
    counts = jnp.bincount(flat_e, length=N_EXPERTS)
    padded = (counts + MOE_TM - 1) // MOE_TM * MOE_TM
    start = jnp.cumsum(counts) - counts
    end_pad = jnp.cumsum(padded)
    start_pad = end_pad - padded
    dest_sorted = start_pad[e_sorted] + jnp.arange(n_assign) - start[e_sorted]
    dest = jnp.zeros((n_assign,), jnp.int32).at[order].set(dest_sorted.astype(jnp.int32))
    n_blocks = n_assign // MOE_TM + N_EXPERTS
    blk_e = jnp.minimum(jnp.searchsorted(end_pad, jnp.arange(n_blocks) * MOE_TM, side='right'), N_EXPERTS - 1)
    n_used = (end_pad[-1] // MOE_TM).astype(jnp.int32).reshape(1)
    return dest, blk_e.astype(jnp.int32), n_used, n_blocks * MOE_TM


def _moe(x1, top_idx, gates, w_gu, b_gu, w_dn, b_dn, g, b, dn_alpha):
    dest, blk_e, n_used, n_rows = _route(top_idx)
    dest3 = dest.reshape(-1, 1, TOK_BLOCK * TOP_K)
    xs = _scatter_rows(dest3, x1, jnp.zeros((n_rows, D_MODEL), F32))
    ys = _experts(blk_e, n_used, xs, w_gu, b_gu, w_dn, b_dn)
    return _combine(dest3, gates, x1, g, b, ys, dn_alpha)


def _project_all(x, w_in, pos, tm_plain, tm_rope):
    ret_tab = [jnp.concatenate(p, axis=1) for p in zip(_rope_table(pos, R_DK, R_HEADS),
                                                       _rope_table(pos, R_DK, R_HEADS, R_DK ** -0.5))]
    diff_tab = _rope_table(pos, D_DH, 2 * D_HEADS)
    qk = _proj(x, w_in, 0, COL, tm_rope, (ret_tab[0], ret_tab[1], R_DK))
    vr = _proj(x, w_in, 1 * COL, COL, tm_plain)
    gr = _proj(x, w_in, 2 * COL, COL, tm_plain)
    qd = _proj(x, w_in, 3 * COL, COL, tm_rope, (diff_tab[0], diff_tab[1], D_DH))
    kd = _proj(x, w_in, 4 * COL, COL, tm_rope, (diff_tab[0], diff_tab[1], D_DH))
    vd = _proj(x, w_in, 5 * COL, COL, tm_plain)
    ga = _proj(x, w_in, 6 * COL, COL, tm_plain)
    gb = _proj(x, w_in, 7 * COL, COL, tm_plain)
    return qk, vr, gr, qd, kd, vd, ga, gb


def kernel(x_prompt, x_sample, cache_k, cache_v, state_ret, page_table, w_in, w_branch_ret, w_branch_diff, w_out, lam_q1, lam_k1, lam_q2, lam_k2, subln_w, ln1_g, ln1_b, w_router, b_router, w_gate_up, b_gate_up, w_down, b_down, ln2_g, ln2_b):
    depth = w_in.shape[0]
    assert depth == 1, "single-layer trunk"
    batch, seq, _ = x_prompt.shape
    dbatch, dseq, _ = x_sample.shape
    dn_alpha = (2.0 * depth) ** 0.25
    lam_init = 0.8 - 0.6 * math.exp(-0.3 * 0)
    n_p, n_s = batch * seq, dbatch * dseq

    w_in_b = w_in[0].astype(BF16)
    wr_b, wd_b, wo_b = w_branch_ret[0].astype(BF16), w_branch_diff[0].astype(BF16), w_out[0].astype(BF16)
    w_rt_b = w_router[0].astype(BF16)
    w_gu_b, w_dn_b = w_gate_up[0].astype(BF16), w_down[0].astype(BF16)
    lam_args = (lam_q1, lam_k1, lam_q2, lam_k2, subln_w)

    xp = x_prompt.reshape(n_p, D_MODEL)
    qk, vr, gr, qd, kd, vd, ga, gb = _project_all(xp, w_in_b, jnp.arange(seq), 1024, 512)
    o_r, s_p = _retention_prompt(qk, vr, gr, batch, seq)
    o_d = _dattn_prompt(lam_args, qd, kd, vd, batch, seq, lam_init)
    x1_p, idx_p, gate_p = _mix(xp, o_r, o_d, ga, gb, wr_b, wd_b, wo_b, ln1_g, ln1_b, w_rt_b, b_router, 256, dn_alpha)

    xs_ = x_sample.reshape(n_s, D_MODEL)
    pos_s = jnp.tile(PAST_LEN + jnp.arange(dseq), dbatch)
    qk_s, vr_s, gr_s, qd_s, kd_s, vd_s, ga_s, gb_s = _project_all(xs_, w_in_b, pos_s, n_s, n_s)
    pad8 = lambda a: jnp.pad(a.reshape(dbatch, dseq, -1), ((0, 0), (0, 8 - dseq), (0, 0)))
    pad_page = lambda a: jnp.pad(a.reshape(dbatch, dseq, -1), ((0, 0), (0, PAGE_SIZE - dseq), (0, 0)))
    o_r_s, s_s = _retention_sample(pad8(qk_s), pad8(vr_s), pad8(gr_s), state_ret[0], dseq)
    n_pool = cache_k.shape[1]
    o_d_s = _dattn_decode(lam_args, page_table, pad8(qd_s), pad_page(kd_s), pad_page(vd_s),
                          cache_k[0].reshape(n_pool, PAGE_SIZE, -1), cache_v[0].reshape(n_pool, PAGE_SIZE, -1),
                          lam_init)
    o_r_s = o_r_s[:, :dseq].reshape(n_s, -1)
    o_d_s = o_d_s[:, :dseq].reshape(n_s, -1)
    x1_s, idx_s, gate_s = _mix(xs_, o_r_s, o_d_s, ga_s, gb_s, wr_b, wd_b, wo_b, ln1_g, ln1_b, w_rt_b, b_router,
                               n_s, dn_alpha)

    x1 = jnp.concatenate([x1_p, x1_s], axis=0)
    top_idx = jnp.concatenate([idx_p, idx_s], axis=0)[:, :TOP_K]
    gates = jnp.concatenate([gate_p, gate_s], axis=0)
    y = _moe(x1, top_idx, gates, w_gu_b, b_gate_up[0][:, None, :], w_dn_b, b_down[0][:, None, :],
             ln2_g, ln2_b, dn_alpha)

    return (y[:n_p].reshape(batch, seq, D_MODEL),
            y[n_p:].reshape(dbatch, dseq, D_MODEL),
            kd.reshape(1, batch, seq, D_HEADS, 2, D_DH),
            vd.reshape(1, batch, seq, D_HEADS, D_DV),
            s_p[None],
            kd_s.reshape(1, dbatch, dseq, D_HEADS, 2, D_DH),
            vd_s.reshape(1, dbatch, dseq, D_HEADS, D_DV),
            s_s[None])
```

```python
import functools
import math

import jax
import jax.numpy as jnp
from jax import lax
from jax.experimental import pallas as pl
from jax.experimental.pallas import tpu as pltpu

F32 = jnp.float32
BF16 = jnp.bfloat16

D_MODEL = 1024
PAST_LEN = 16384
PAGE_SIZE = 128
R_HEADS, R_DK, R_DV, R_CHUNK = 4, 128, 256, 128
D_HEADS, D_DH, D_DV = 8, 64, 128
ROPE_THETA = 10000.0
N_EXPERTS, TOP_K, D_FF = 32, 4, 1024
SWIGLU_ALPHA, SWIGLU_LIMIT = 1.702, 7.0
NEG_INF = -1e30

LANE = 128
VMEM_LIMIT = 56 * 1024 * 1024
ATTN_BLOCK = 512
DEC_PAGES = 8
MOE_TM = 512
TOK_BLOCK = 128
COL = 1024


def _cparams(*sem):
    return pltpu.CompilerParams(dimension_semantics=sem, vmem_limit_bytes=VMEM_LIMIT)


def _dot(a, b):
    return jnp.dot(a, b, preferred_element_type=F32)


def _dot_nt(a, b):
    return lax.dot_general(a, b, (((1,), (1,)), ((), ())), preferred_element_type=F32)


def _swap_halves(h, unit):
    outs = []
    lane = lax.broadcasted_iota(jnp.int32, (h.shape[0], LANE), 1)
    for g in range(h.shape[1] // LANE):
        sl = h[:, g * LANE:(g + 1) * LANE]
        if unit == LANE:
            outs.append(pltpu.roll(sl, LANE // 2, 1))
        else:
            fwd = pltpu.roll(sl, unit // 2, 1)
            bwd = pltpu.roll(sl, LANE - unit // 2, 1)
            outs.append(jnp.where((lane % unit) < unit // 2, bwd, fwd))
    return jnp.concatenate(outs, axis=1)


def _proj_kernel(x_ref, w_ref, o_ref):
    o_ref[...] = _dot(x_ref[...].astype(BF16), w_ref[...])


def _proj_rope_kernel(x_ref, w_ref, cos_ref, sin_ref, o_ref, *, unit):
    h = _dot(x_ref[...].astype(BF16), w_ref[...])
    o_ref[...] = h * cos_ref[...] + _swap_halves(h, unit) * sin_ref[...]


def _proj_rope_t_kernel(x_ref, w_ref, cos_ref, sin_ref, o_ref):
    ht = _dot(x_ref[...].astype(BF16), w_ref[...]).T
    half = D_DH // 2
    cos, sin = cos_ref[...], sin_ref[...]
    parts = []
    for u in range(ht.shape[0] // D_DH):
        blk = ht[u * D_DH:(u + 1) * D_DH]
        swapped = jnp.concatenate([blk[half:], blk[:half]], axis=0)
        parts.append(blk * cos + swapped * sin)
    o_ref[...] = jnp.concatenate(parts, axis=0)


def _proj_keys_t(x, w, col0, tm, cos_t, sin_t, batch, seq):
    n_pos = seq // tm
    return pl.pallas_call(
        _proj_rope_t_kernel, grid=(batch * n_pos,),
        in_specs=[pl.BlockSpec((tm, D_MODEL), lambda i: (i, 0)),
                  pl.BlockSpec((D_MODEL, COL), lambda i: (0, col0 // COL)),
                  pl.BlockSpec((D_DH, tm), lambda i: (0, i % n_pos)),
                  pl.BlockSpec((D_DH, tm), lambda i: (0, i % n_pos))],
        out_specs=pl.BlockSpec((COL, tm), lambda i: (i // n_pos, i % n_pos)),
        out_shape=jax.ShapeDtypeStruct((batch * COL, seq), F32),
        compiler_params=_cparams("parallel"), name="proj_keys_t",
    )(x, w, cos_t, sin_t)


def _proj(x, w, col0, ncols, tm, rope=None):
    t = x.shape[0]
    grid = (t // tm, ncols // COL)
    in_specs = [pl.BlockSpec((tm, D_MODEL), lambda i, j: (i, 0)),
                pl.BlockSpec((D_MODEL, COL), lambda i, j: (0, col0 // COL + j))]
    args = [x, w]
    if rope is None:
        body = _proj_kernel
    else:
        cos, sin, unit = rope
        n_pos = cos.shape[0] // tm
        in_specs += [pl.BlockSpec((tm, COL), lambda i, j: (i % n_pos, j))] * 2
        args += [cos, sin]
        body = functools.partial(_proj_rope_kernel, unit=unit)
    return pl.pallas_call(
        body, grid=grid, in_specs=in_specs,
        out_specs=pl.BlockSpec((tm, COL), lambda i, j: (i, j)),
        out_shape=jax.ShapeDtypeStruct((t, ncols), F32),
        compiler_params=_cparams("parallel", "arbitrary"), name="proj" if rope is None else "proj_rope",
    )(*args)


def _rope_table(pos, unit, n_units, scale=1.0):
    half = unit // 2
    inv = ROPE_THETA ** (-jnp.arange(half, dtype=F32) / half)
    ang = pos.astype(F32)[:, None] * inv[None, :]
    cos, sin = jnp.cos(ang), jnp.sin(ang)
    c = jnp.concatenate([cos, cos], axis=-1) * scale
    s = jnp.concatenate([-sin, sin], axis=-1) * scale
    return jnp.tile(c, (1, n_units)), jnp.tile(s, (1, n_units))


def _ret_tables(chunk, n_tok):
    log_gamma = jnp.log(1.0 - 2.0 ** (-5.0 - jnp.arange(R_HEADS, dtype=F32)))
    idx = jnp.arange(chunk, dtype=F32)
    diff = idx[:, None] - idx[None, :]
    causal = diff >= 0
    dec = jnp.where(causal[None], jnp.exp(log_gamma[:, None, None] * jnp.where(causal, diff, 0.0)[None]), 0.0)
    qdec = jnp.exp(log_gamma[:, None] * (idx[None, :] + 1.0))
    kdec = jnp.exp(log_gamma[:, None] * (n_tok - 1.0 - idx[None, :]))
    sdec = jnp.exp(log_gamma * n_tok)
    return (dec,
            jnp.broadcast_to(qdec[:, :, None], (R_HEADS, chunk, R_DV)),
            jnp.broadcast_to(kdec[:, :, None], (R_HEADS, chunk, R_DK)),
            jnp.broadcast_to(sdec[:, None, None], (R_HEADS, 1, R_DV)))


def _ret_chunk(q, k, v, g, state, dec, qdec, kdec, sdec):
    qb, kb, vb = q.astype(BF16), k.astype(BF16), v.astype(BF16)
    scores = _dot_nt(qb, kb) * dec
    inner = _dot(scores.astype(BF16), vb)
    cross = _dot(qb, state.astype(BF16)) * qdec
    o = inner + cross
    kd_t = (k * kdec).T.astype(BF16)
    s_new = state * sdec + _dot(kd_t, vb)
    mu = jnp.mean(o, axis=-1, keepdims=True)
    var = jnp.mean(jnp.square(o - mu), axis=-1, keepdims=True)
    o = (o - mu) * lax.rsqrt(var + 1e-6)
    return o * (g * jax.nn.sigmoid(g)), s_new


def _ret_prompt_kernel(q_ref, k_ref, v_ref, g_ref, dec_ref, qdec_ref, kdec_ref, sdec_ref, o_ref, s_ref):
    @pl.when(pl.program_id(1) == 0)
    def _():
        s_ref[...] = jnp.zeros_like(s_ref)

    for h in range(R_HEADS):
        dk, dv = slice(h * R_DK, (h + 1) * R_DK), slice(h * R_DV, (h + 1) * R_DV)
        o, s_new = _ret_chunk(q_ref[:, dk], k_ref[:, dk], v_ref[:, dv], g_ref[:, dv], s_ref[0, h],
                              dec_ref[h], qdec_ref[h], kdec_ref[h], sdec_ref[h])
        o_ref[:, dv] = o
        s_ref[0, h] = s_new


def _retention_prompt(qk, vr, gr, batch, seq):
    nc = seq // R_CHUNK
    tabs = _ret_tables(R_CHUNK, R_CHUNK)
    row = lambda b, c: b * nc + c
    whole = lambda a: pl.BlockSpec(a.shape, lambda b, c: (0,) * a.ndim)
    return pl.pallas_call(
        _ret_prompt_kernel, grid=(batch, nc),
        in_specs=[
            pl.BlockSpec((R_CHUNK, R_HEADS * R_DK), lambda b, c: (row(b, c), 0)),
            pl.BlockSpec((R_CHUNK, R_HEADS * R_DK), lambda b, c: (row(b, c), 1)),
            pl.BlockSpec((R_CHUNK, R_HEADS * R_DV), lambda b, c: (row(b, c), 0)),
            pl.BlockSpec((R_CHUNK, R_HEADS * R_DV), lambda b, c: (row(b, c), 0)),
        ] + [whole(a) for a in tabs],
        out_specs=[
            pl.BlockSpec((R_CHUNK, R_HEADS * R_DV), lambda b, c: (row(b, c), 0)),
            pl.BlockSpec((1, R_HEADS, R_DK, R_DV), lambda b, c: (b, 0, 0, 0)),
        ],
        out_shape=[jax.ShapeDtypeStruct((batch * seq, R_HEADS * R_DV), F32),
                   jax.ShapeDtypeStruct((batch, R_HEADS, R_DK, R_DV), F32)],
        compiler_params=_cparams("parallel", "arbitrary"), name="retention_prompt",
    )(qk, qk, vr, gr, *tabs)


def _ret_sample_kernel(q_ref, k_ref, v_ref, g_ref, s0_ref, dec_ref, qdec_ref, kdec_ref, sdec_ref,
                       o_ref, s_ref, qp, kp, vp, gp):
    n = q_ref.shape[1]
    for pad, src in ((qp, q_ref), (kp, k_ref), (vp, v_ref), (gp, g_ref)):
        pad[...] = jnp.zeros_like(pad)
        pad[0:n, :] = src[0]
    o, s_new = _ret_chunk(qp[...], kp[...], vp[...], gp[...], s0_ref[0, 0],
                          dec_ref[0], qdec_ref[0], kdec_ref[0], sdec_ref[0])
    o_ref[0] = o[0:n, :]
    s_ref[0, 0] = s_new


def _retention_sample(qk, vr, gr, state, n_tok):
    batch, rows, _ = qk.shape
    tabs = _ret_tables(R_CHUNK, n_tok)
    return pl.pallas_call(
        _ret_sample_kernel, grid=(batch, R_HEADS),
        in_specs=[
            pl.BlockSpec((1, rows, R_DK), lambda b, h: (b, 0, h)),
            pl.BlockSpec((1, rows, R_DK), lambda b, h: (b, 0, R_HEADS + h)),
            pl.BlockSpec((1, rows, R_DV), lambda b, h: (b, 0, h)),
            pl.BlockSpec((1, rows, R_DV), lambda b, h: (b, 0, h)),
            pl.BlockSpec((1, 1, R_DK, R_DV), lambda b, h: (b, h, 0, 0)),
            pl.BlockSpec((1, R_CHUNK, R_CHUNK), lambda b, h: (h, 0, 0)),
            pl.BlockSpec((1, R_CHUNK, R_DV), lambda b, h: (h, 0, 0)),
            pl.BlockSpec((1, R_CHUNK, R_DK), lambda b, h: (h, 0, 0)),
            pl.BlockSpec((1, 1, R_DV), lambda b, h: (h, 0, 0)),
        ],
        out_specs=[
            pl.BlockSpec((1, rows, R_DV), lambda b, h: (b, 0, h)),
            pl.BlockSpec((1, 1, R_DK, R_DV), lambda b, h: (b, h, 0, 0)),
        ],
        out_shape=[jax.ShapeDtypeStruct((batch, rows, R_HEADS * R_DV), F32),
                   jax.ShapeDtypeStruct((batch, R_HEADS, R_DK, R_DV), F32)],
        scratch_shapes=[pltpu.VMEM((R_CHUNK, R_DK), F32), pltpu.VMEM((R_CHUNK, R_DK), F32),
                        pltpu.VMEM((R_CHUNK, R_DV), F32), pltpu.VMEM((R_CHUNK, R_DV), F32)],
        compiler_params=_cparams("parallel", "parallel"), name="retention_sample",
    )(qk, qk, vr, gr, state, *tabs)


def _lambda(lq1_ref, lk1_ref, lq2_ref, lk2_ref, lam_init):
    a = jnp.sum(lq1_ref[...] * lk1_ref[...], axis=-1, keepdims=True)
    b = jnp.sum(lq2_ref[...] * lk2_ref[...], axis=-1, keepdims=True)
    return jnp.exp(a) - jnp.exp(b) + lam_init


def _sub_rms(o, w, lam_init):
    ms = jnp.mean(jnp.square(o), axis=-1, keepdims=True)
    return o * lax.rsqrt(ms + 1e-5) * w * (1.0 - lam_init)


def _dattn_prompt_kernel(lq1_ref, lk1_ref, lq2_ref, lk2_ref, w_ref, q_ref, kt_ref, v_ref, o_ref,
                         kc_ref, vx_ref, *, blk, lam_init):
    qi = pl.program_id(2)

    @pl.when(qi == 0)
    def _():
        kc_ref[...] = kt_ref[...].astype(BF16)
        vx_ref[:, 0:D_DV] = v_ref[...].astype(BF16)
        vx_ref[:, D_DV:2 * D_DV] = jnp.ones((vx_ref.shape[0], D_DV), BF16)

    q = q_ref[...] * (D_DH ** -0.5)
    qs = [q[:, c * D_DH:(c + 1) * D_DH].astype(BF16) for c in range(2)]
    row = lax.broadcasted_iota(jnp.int32, (blk, blk), 0)
    col = lax.broadcasted_iota(jnp.int32, (blk, blk), 1)

    def step(j, carry, masked):
        start = pl.multiple_of(j * blk, blk)
        vx = vx_ref[pl.ds(start, blk), :]
        out = []
        for c in range(2):
            m, acc = carry[2 * c], carry[2 * c + 1]
            s = _dot(qs[c], kc_ref[c * D_DH:(c + 1) * D_DH, pl.ds(start, blk)])
            if masked:
                s = jnp.where(col <= row, s, NEG_INF)
            m_new = jnp.maximum(m, jnp.max(s, axis=-1, keepdims=True))
            p = jnp.exp(s - m_new)
            out += [m_new, acc * jnp.exp(m - m_new) + _dot(p.astype(BF16), vx)]
        return tuple(out)

    carry = (jnp.full((blk, 1), NEG_INF, F32), jnp.zeros((blk, 2 * D_DV), F32)) * 2
    carry = lax.fori_loop(0, qi, functools.partial(step, masked=False), carry)
    carry = step(qi, carry, True)
    outs = [carry[2 * c + 1][:, 0:D_DV] / carry[2 * c + 1][:, D_DV:D_DV + 1] for c in range(2)]
    lam = _lambda(lq1_ref, lk1_ref, lq2_ref, lk2_ref, lam_init)
    o_ref[...] = _sub_rms(outs[0] - lam * outs[1], w_ref[...], lam_init)


def _lam_specs(n):
    zero = lambda *_: (0, 0)
    return [pl.BlockSpec((1, D_DH), zero)] * 4 + [pl.BlockSpec((1, D_DV), zero)]


def _dattn_prompt(lam_args, qd, kd_t, vd, batch, seq, lam_init):
    blk = min(ATTN_BLOCK, seq)
    nq = seq // blk
    return pl.pallas_call(
        functools.partial(_dattn_prompt_kernel, blk=blk, lam_init=lam_init),
        grid=(batch, D_HEADS, nq),
        in_specs=_lam_specs(3) + [
            pl.BlockSpec((blk, 2 * D_DH), lambda b, h, i: (b * nq + i, h)),
            pl.BlockSpec((2 * D_DH, seq), lambda b, h, i: (b * D_HEADS + h, 0)),
            pl.BlockSpec((seq, D_DV), lambda b, h, i: (b, h)),
        ],
        out_specs=pl.BlockSpec((blk, D_DV), lambda b, h, i: (b * nq + i, h)),
        out_shape=jax.ShapeDtypeStruct((batch * seq, D_HEADS * D_DV), F32),
        scratch_shapes=[pltpu.VMEM((2 * D_DH, seq), BF16), pltpu.VMEM((seq, 2 * D_DV), BF16)],
        compiler_params=_cparams("parallel", "parallel", "arbitrary"), name="dattn_prompt",
    )(*lam_args, qd, kd_t, vd)


DEC_ROWS = D_HEADS * 2 * 8


def _dattn_decode_kernel(pt_ref, lq1_ref, lk1_ref, lq2_ref, lk2_ref, w_ref, q_ref, kn_ref, vn_ref, *rest,
                         n_pages, lam_init):
    k_refs, v_refs = rest[:n_pages], rest[n_pages:2 * n_pages]
    o_ref, qbd_ref, m_ref, l_ref, acc_ref = rest[2 * n_pages:]
    j = pl.program_id(1)

    @pl.when(j == 0)
    def _():
        q8 = q_ref[0] * (D_DH ** -0.5)
        tiled = jnp.concatenate([q8] * (DEC_ROWS // 8), axis=0)
        r = lax.broadcasted_iota(jnp.int32, tiled.shape, 0)
        cidx = lax.broadcasted_iota(jnp.int32, tiled.shape, 1)
        qbd_ref[...] = jnp.where(r // 8 == cidx // D_DH, tiled, 0.0).astype(BF16)
        m_ref[...] = jnp.full(m_ref.shape, NEG_INF, F32)
        l_ref[...] = jnp.zeros(l_ref.shape, F32)
        acc_ref[...] = jnp.zeros(acc_ref.shape, F32)

    rows_h = DEC_ROWS // D_HEADS

    def update(s_list, v_pages):
        m = m_ref[...]
        m_new = m
        for s in s_list:
            m_new = jnp.maximum(m_new, jnp.max(s, axis=-1, keepdims=True))
        alpha = jnp.exp(m - m_new)
        p = jnp.concatenate([jnp.exp(s - m_new) for s in s_list], axis=1)
        l_ref[...] = l_ref[...] * alpha + jnp.sum(p, axis=-1, keepdims=True)
        m_ref[...] = m_new
        p = p.astype(BF16)
        for h in range(D_HEADS):
            rows = slice(h * rows_h, (h + 1) * rows_h)
            v_h = jnp.concatenate([v[0, pl.ds(h, PAGE_SIZE, stride=D_HEADS), :] for v in v_pages], axis=0)
            acc_ref[rows, :] = acc_ref[rows, :] * alpha[rows, :] + _dot(p[rows, :], v_h.astype(BF16))

    qbd = qbd_ref[...]
    update([_dot(qbd, k[0].astype(BF16)) for k in k_refs], v_refs)

    @pl.when(j == pl.num_programs(1) - 1)
    def _():
        s = _dot(qbd, kn_ref[0].astype(BF16))
        t_q = lax.broadcasted_iota(jnp.int32, s.shape, 0) % 8
        t_k = lax.broadcasted_iota(jnp.int32, s.shape, 1)
        update([jnp.where(t_k <= t_q, s, NEG_INF)], [vn_ref])
        acc = acc_ref[...] / l_ref[...]
        lam = _lambda(lq1_ref, lk1_ref, lq2_ref, lk2_ref, lam_init)
        w = w_ref[...]
        heads = []
        for h in range(D_HEADS):
            o = acc[h * rows_h:h * rows_h + 8, :] - lam * acc[h * rows_h + 8:(h + 1) * rows_h, :]
            heads.append(_sub_rms(o, w, lam_init))
        o_ref[0] = jnp.concatenate(heads, axis=1)


def _dattn_decode(lam_args, page_table, q8, kn_t, vn, cache_kt, cache_v, lam_init):
    batch, n_pages = page_table.shape
    steps = n_pages // DEC_PAGES
    width = q8.shape[-1]
    page = (1,) + cache_kt.shape[1:]

    def page_spec(p):
        return pl.BlockSpec(page, lambda b, j, pt: (pt[b * n_pages + j * DEC_PAGES + p], 0, 0))

    zero = lambda b, j, pt: (0, 0)
    seq_spec = lambda shape: pl.BlockSpec(shape, lambda b, j, pt: (b, 0, 0))
    grid_spec = pltpu.PrefetchScalarGridSpec(
        num_scalar_prefetch=1, grid=(batch, steps),
        in_specs=([pl.BlockSpec((1, D_DH), zero)] * 4 + [pl.BlockSpec((1, D_DV), zero)]
                  + [seq_spec((1, 8, width)), seq_spec(page), seq_spec(page)]
                  + [page_spec(p) for p in range(DEC_PAGES)] * 2),
        out_specs=seq_spec((1, 8, width)),
        scratch_shapes=[pltpu.VMEM((DEC_ROWS, width), BF16), pltpu.VMEM((DEC_ROWS, 1), F32),
                        pltpu.VMEM((DEC_ROWS, 1), F32), pltpu.VMEM((DEC_ROWS, D_DV), F32)])
    return pl.pallas_call(
        functools.partial(_dattn_decode_kernel, n_pages=DEC_PAGES, lam_init=lam_init),
        grid_spec=grid_spec,
        out_shape=jax.ShapeDtypeStruct((batch, 8, width), F32),
        compiler_params=_cparams("parallel", "arbitrary"), name="dattn_decode",
    )(page_table.reshape(-1), *lam_args, q8, kn_t, vn, *([cache_kt] * DEC_PAGES), *([cache_v] * DEC_PAGES))


def _layer_norm(y, g, b):
    mu = jnp.mean(y, axis=-1, keepdims=True)
    var = jnp.mean(jnp.square(y - mu), axis=-1, keepdims=True)
    return (y - mu) * lax.rsqrt(var + 1e-5) * g + b


def _mix_kernel(x_ref, or_ref, od_ref, ga_ref, gb_ref, wr_ref, wd_ref, wo_ref, g_ref, b_ref, wrt_ref, brt_ref,
                x1_ref, idx_ref, gate_ref, *, dn_alpha):
    r = _dot(or_ref[...].astype(BF16), wr_ref[...])
    d = _dot(od_ref[...].astype(BF16), wd_ref[...])
    mix = jax.nn.sigmoid(ga_ref[...]) * r + jax.nn.sigmoid(gb_ref[...]) * d
    y = dn_alpha * x_ref[...] + _dot(mix.astype(BF16), wo_ref[...])
    x1 = _layer_norm(y, g_ref[...], b_ref[...])
    x1_ref[...] = x1
    vals = _dot(x1.astype(BF16), wrt_ref[...]) + brt_ref[...]
    col = lax.broadcasted_iota(jnp.int32, vals.shape, 1)
    lane = lax.broadcasted_iota(jnp.int32, idx_ref.shape, 1)
    idx_out = jnp.zeros(idx_ref.shape, jnp.int32)
    val_out = jnp.full(gate_ref.shape, NEG_INF, F32)
    for k in range(TOP_K):
        mx = jnp.max(vals, axis=-1, keepdims=True)
        first = jnp.min(jnp.where(vals == mx, col, N_EXPERTS), axis=-1, keepdims=True)
        idx_out = jnp.where(lane == k, first, idx_out)
        val_out = jnp.where(lane == k, mx, val_out)
        vals = jnp.where(col == first, NEG_INF, vals)
    e = jnp.exp(val_out - jnp.max(val_out, axis=-1, keepdims=True))
    idx_ref[...] = idx_out
    gate_ref[...] = e / jnp.sum(e, axis=-1, keepdims=True)


def _mix(x, o_r, o_d, ga, gb, wr, wd, wo, g, b, w_rt, b_rt, tm, dn_alpha):
    t = x.shape[0]
    tok = pl.BlockSpec((tm, D_MODEL), lambda i: (i, 0))
    mat = pl.BlockSpec((D_MODEL, D_MODEL), lambda i: (0, 0))
    vec = pl.BlockSpec((1, D_MODEL), lambda i: (0, 0))
    return pl.pallas_call(
        functools.partial(_mix_kernel, dn_alpha=dn_alpha), grid=(t // tm,),
        in_specs=[tok] * 5 + [mat] * 3 + [vec, vec,
                                          pl.BlockSpec((D_MODEL, N_EXPERTS), lambda i: (0, 0)),
                                          pl.BlockSpec((1, N_EXPERTS), lambda i: (0, 0))],
        out_specs=[tok, pl.BlockSpec((tm, LANE), lambda i: (i, 0)), pl.BlockSpec((tm, LANE), lambda i: (i, 0))],
        out_shape=[jax.ShapeDtypeStruct((t, D_MODEL), F32), jax.ShapeDtypeStruct((t, LANE), jnp.int32),
                   jax.ShapeDtypeStruct((t, LANE), F32)],
        compiler_params=_cparams("parallel"), name="mix_ln_router",
    )(x, o_r, o_d, ga, gb, wr, wd, wo, g, b, w_rt, b_rt)


def _scatter_rows_kernel(dest_ref, x_ref, xs_in_ref, xs_ref, sem):
    del xs_in_ref
    n = x_ref.shape[0]

    def copy(t, k):
        return pltpu.make_async_copy(x_ref.at[pl.ds(t, 1)], xs_ref.at[pl.ds(dest_ref[0, 0, t * TOP_K + k], 1)], sem)

    def start(t, c):
        for k in range(TOP_K):
            copy(t, k).start()
        return c

    def wait(t, c):
        for k in range(TOP_K):
            copy(t, k).wait()
        return c

    lax.fori_loop(0, n, start, 0)
    lax.fori_loop(0, n, wait, 0)


def _scatter_rows(dest3, x, xs):
    t = x.shape[0]
    return pl.pallas_call(
        _scatter_rows_kernel, grid=(t // TOK_BLOCK,),
        in_specs=[pl.BlockSpec((1, 1, TOK_BLOCK * TOP_K), lambda i: (i, 0, 0), memory_space=pltpu.SMEM),
                  pl.BlockSpec((TOK_BLOCK, D_MODEL), lambda i: (i, 0)),
                  pl.BlockSpec(memory_space=pl.ANY)],
        out_specs=pl.BlockSpec(memory_space=pl.ANY),
        out_shape=jax.ShapeDtypeStruct(xs.shape, xs.dtype),
        scratch_shapes=[pltpu.SemaphoreType.DMA],
        input_output_aliases={2: 0},
        compiler_params=_cparams("arbitrary"), name="moe_scatter_rows",
    )(dest3, x, xs)


def _expert_kernel(blk_e_ref, n_used_ref, x_ref, wgu_ref, bgu_ref, wdn_ref, bdn_ref, y_ref):
    del blk_e_ref
    used = pl.program_id(0) < n_used_ref[0]

    @pl.when(jnp.logical_not(used))
    def _():
        y_ref[...] = jnp.zeros_like(y_ref)

    @pl.when(used)
    def _():
        hgu = _dot(x_ref[...].astype(BF16), wgu_ref[0]) + bgu_ref[0]
        glu = jnp.minimum(hgu[:, 0:D_FF], SWIGLU_LIMIT)
        lin = jnp.clip(hgu[:, D_FF:2 * D_FF], -SWIGLU_LIMIT, SWIGLU_LIMIT)
        act = glu * jax.nn.sigmoid(SWIGLU_ALPHA * glu) * (lin + 1.0)
        y_ref[...] = _dot(act.astype(BF16), wdn_ref[0]) + bdn_ref[0]


def _experts(blk_e, n_used, xs, w_gu, b_gu, w_dn, b_dn):
    n_rows = xs.shape[0]
    blk = lambda i, be, nu: jnp.minimum(i, nu[0] - 1)
    exp = lambda i, be, nu: be[jnp.minimum(i, nu[0] - 1)]
    grid_spec = pltpu.PrefetchScalarGridSpec(
        num_scalar_prefetch=2, grid=(n_rows // MOE_TM,),
        in_specs=[pl.BlockSpec((MOE_TM, D_MODEL), lambda i, be, nu: (blk(i, be, nu), 0)),
                  pl.BlockSpec((1, D_MODEL, 2 * D_FF), lambda i, be, nu: (exp(i, be, nu), 0, 0)),
                  pl.BlockSpec((1, 1, 2 * D_FF), lambda i, be, nu: (exp(i, be, nu), 0, 0)),
                  pl.BlockSpec((1, D_FF, D_MODEL), lambda i, be, nu: (exp(i, be, nu), 0, 0)),
                  pl.BlockSpec((1, 1, D_MODEL), lambda i, be, nu: (exp(i, be, nu), 0, 0))],
        out_specs=pl.BlockSpec((MOE_TM, D_MODEL), lambda i, be, nu: (i, 0)))
    return pl.pallas_call(
        _expert_kernel, grid_spec=grid_spec,
        out_shape=jax.ShapeDtypeStruct((n_rows, D_MODEL), F32),
        compiler_params=_cparams("arbitrary"), name="moe_experts",
    )(blk_e, n_used, xs, w_gu, b_gu, w_dn, b_dn)


def _combine_kernel(dest_ref, gate_ref, x_ref, g_ref, b_ref, ys_ref, o_ref, buf, sem, *, dn_alpha):
    n = x_ref.shape[0]

    def copy(t, k):
        return pltpu.make_async_copy(ys_ref.at[pl.ds(dest_ref[0, 0, t * TOP_K + k], 1)], buf.at[k, pl.ds(t, 1)], sem)

    def start(t, c):
        for k in range(TOP_K):
            copy(t, k).start()
        return c

    def wait(t, c):
        for k in range(TOP_K):
            copy(t, k).wait()
        return c

    lax.fori_loop(0, n, start, 0)
    lax.fori_loop(0, n, wait, 0)
    gate = gate_ref[...]
    y = dn_alpha * x_ref[...]
    for k in range(TOP_K):
        y = y + gate[:, k:k + 1] * buf[k]
    o_ref[...] = _layer_norm(y, g_ref[...], b_ref[...])


def _combine(dest3, gates, x1, g, b, ys, dn_alpha):
    t = x1.shape[0]
    tok = pl.BlockSpec((TOK_BLOCK, D_MODEL), lambda i: (i, 0))
    vec = pl.BlockSpec((1, D_MODEL), lambda i: (0, 0))
    return pl.pallas_call(
        functools.partial(_combine_kernel, dn_alpha=dn_alpha), grid=(t // TOK_BLOCK,),
        in_specs=[pl.BlockSpec((1, 1, TOK_BLOCK * TOP_K), lambda i: (i, 0, 0), memory_space=pltpu.SMEM),
                  pl.BlockSpec((TOK_BLOCK, LANE), lambda i: (i, 0)), tok, vec, vec,
                  pl.BlockSpec(memory_space=pl.ANY)],
        out_specs=tok,
        out_shape=jax.ShapeDtypeStruct((t, D_MODEL), F32),
        scratch_shapes=[pltpu.VMEM((TOP_K, TOK_BLOCK, D_MODEL), F32), pltpu.SemaphoreType.DMA],
        compiler_params=_cparams("arbitrary"), name="moe_combine_ln",
    )(dest3, gates, x1, g, b, ys)


def _onehots(idx):
    lane = lax.broadcasted_iota(jnp.int32, idx.shape, 1)
    return lane, [lane == idx[:, k:k + 1] for k in range(TOP_K)]


def _lanes(cols, lane):
    out = jnp.zeros(lane.shape, jnp.int32)
    for k, c in enumerate(cols):
        out = jnp.where(lane == k, c, out)
    return out


def _rank_kernel(cnt0_ref, idx_ref, rank_ref, cnt_ref, carry_ref):
    @pl.when(pl.program_id(0) == 0)
    def _():
        carry_ref[...] = cnt0_ref[...]

    lane, hots = _onehots(idx_ref[...])
    n = lane.shape[0]
    chose = sum(h.astype(F32) for h in hots)
    r = lax.broadcasted_iota(jnp.int32, (n, n), 0)
    c = lax.broadcasted_iota(jnp.int32, (n, n), 1)
    earlier = jnp.where(c < r, 1.0, 0.0).astype(BF16)
    before = _dot(earlier, chose.astype(BF16)) + carry_ref[...]
    ranks = [jnp.sum(jnp.where(h, before, 0.0), axis=-1, keepdims=True).astype(jnp.int32) for h in hots]
    rank_ref[...] = _lanes(ranks, lane)
    carry_ref[...] = carry_ref[...] + jnp.sum(chose, axis=0, keepdims=True)
    cnt_ref[...] = carry_ref[...]


def _rank(cnt0, idx):
    t = idx.shape[0]
    tok = pl.BlockSpec((TOK_BLOCK, LANE), lambda i: (i, 0))
    one = pl.BlockSpec((1, LANE), lambda i: (0, 0))
    return pl.pallas_call(
        _rank_kernel, grid=(t // TOK_BLOCK,), in_specs=[one, tok], out_specs=[tok, one],
        out_shape=[jax.ShapeDtypeStruct((t, LANE), jnp.int32), jax.ShapeDtypeStruct((1, LANE), F32)],
        scratch_shapes=[pltpu.VMEM((1, LANE), F32)],
        compiler_params=_cparams("arbitrary"), name="moe_rank",
    )(cnt0, idx)


def _dest_kernel(cnt_ref, idx_ref, rank_ref, dest_ref, blk_ref, used_ref, start_ref):
    @pl.when(pl.program_id(0) == 0)
    def _():
        lane = lax.broadcasted_iota(jnp.int32, (8, LANE), 1)
        cnt = jnp.broadcast_to(cnt_ref[...], (8, LANE)).astype(jnp.int32)
        nblk = jnp.where(lane < N_EXPERTS, jnp.right_shift(cnt + (MOE_TM - 1), MOE_TM.bit_length() - 1), 0)
        end = nblk
        for s in (1, 2, 4, 8, 16):
            end = end + jnp.where(lane >= s, pltpu.roll(end, s, 1), 0)
        start_ref[...] = ((end - nblk) * MOE_TM).astype(F32)
        used_ref[...] = jnp.broadcast_to(end[:, N_EXPERTS - 1:N_EXPERTS], (8, LANE))
        blk = lax.broadcasted_iota(jnp.int32, blk_ref.shape, 0)
        lane_b = lax.broadcasted_iota(jnp.int32, blk_ref.shape, 1)
        done = jnp.where(lane_b < N_EXPERTS, jnp.where(end[0:1, :] <= blk, 1.0, 0.0), 0.0)
        blk_ref[...] = jnp.broadcast_to(
            jnp.minimum(jnp.sum(done, axis=-1, keepdims=True), N_EXPERTS - 1.0).astype(jnp.int32), blk_ref.shape)

    lane, hots = _onehots(idx_ref[...])
    start = start_ref[0:1, :]
    rank = rank_ref[...]
    dest_ref[...] = _lanes([jnp.sum(jnp.where(h, start, 0.0), axis=-1, keepdims=True).astype(jnp.int32)
                            + rank[:, k:k + 1] for k, h in enumerate(hots)], lane)


def _dest(cnt, idx, rank, n_blocks):
    t = idx.shape[0]
    tok = pl.BlockSpec((TOK_BLOCK, LANE), lambda i: (i, 0))
    rows = -(-n_blocks // 8) * 8
    return pl.pallas_call(
        _dest_kernel, grid=(t // TOK_BLOCK,),
        in_specs=[pl.BlockSpec((1, LANE), lambda i: (0, 0)), tok, tok],
        out_specs=[tok, pl.BlockSpec((rows, LANE), lambda i: (0, 0)), pl.BlockSpec((8, LANE), lambda i: (0, 0))],
        out_shape=[jax.ShapeDtypeStruct((t, LANE), jnp.int32), jax.ShapeDtypeStruct((rows, LANE), jnp.int32),
                   jax.ShapeDtypeStruct((8, LANE), jnp.int32)],
        scratch_shapes=[pltpu.VMEM((8, LANE), F32)],
        compiler_params=_cparams("arbitrary"), name="moe_dest",
    )(cnt, idx, rank)


def _moe(groups, w_gu, b_gu, w_dn, b_dn, g, b, dn_alpha):
    n_assign = sum(x1.shape[0] for x1, _, _ in groups) * TOP_K
    n_blocks = n_assign // MOE_TM + N_EXPERTS
    cnt = jnp.zeros((1, LANE), F32)
    ranks = []
    for _, idx, _ in groups:
        rank, cnt = _rank(cnt, idx)
        ranks.append(rank)
    xs = jnp.zeros((n_blocks * MOE_TM, D_MODEL), F32)
    dests = []
    for (x1, idx, _), rank in zip(groups, ranks):
        dest, blk, used = _dest(cnt, idx, rank, n_blocks)
        dest3 = dest[:, :TOP_K].reshape(-1, 1, TOK_BLOCK * TOP_K)
        xs = _scatter_rows(dest3, x1, xs)
        dests.append(dest3)
    ys = _experts(blk[:n_blocks, 0], used[0, :1], xs, w_gu, b_gu, w_dn, b_dn)
    return [_combine(dest3, gates, x1, g, b, ys, dn_alpha) for (x1, _, gates), dest3 in zip(groups, dests)]


def _project_all(x, w_in, pos, tm_plain, tm_rope, keys_t=None):
    ret_tab = [jnp.concatenate(p, axis=1) for p in zip(_rope_table(pos, R_DK, R_HEADS),
                                                       _rope_table(pos, R_DK, R_HEADS, R_DK ** -0.5))]
    diff_tab = _rope_table(pos, D_DH, 2 * D_HEADS)
    qk = _proj(x, w_in, 0, COL, tm_rope, (ret_tab[0], ret_tab[1], R_DK))
    vr = _proj(x, w_in, 1 * COL, COL, tm_plain)
    gr = _proj(x, w_in, 2 * COL, COL, tm_plain)
    qd = _proj(x, w_in, 3 * COL, COL, tm_rope, (diff_tab[0], diff_tab[1], D_DH))
    if keys_t is None:
        kd = _proj(x, w_in, 4 * COL, COL, tm_rope, (diff_tab[0], diff_tab[1], D_DH))
    else:
        cos_t, sin_t = (t[:, :D_DH].T for t in diff_tab)
        kd = _proj_keys_t(x, w_in, 4 * COL, tm_rope, cos_t, sin_t, *keys_t)
    vd = _proj(x, w_in, 5 * COL, COL, tm_plain)
    ga = _proj(x, w_in, 6 * COL, COL, tm_plain)
    gb = _proj(x, w_in, 7 * COL, COL, tm_plain)
    return qk, vr, gr, qd, kd, vd, ga, gb


def kernel(x_prompt, x_sample, cache_k, cache_v, state_ret, page_table, w_in, w_branch_ret, w_branch_diff, w_out, lam_q1, lam_k1, lam_q2, lam_k2, subln_w, ln1_g, ln1_b, w_router, b_router, w_gate_up, b_gate_up, w_down, b_down, ln2_g, ln2_b):
    depth = w_in.shape[0]
    assert depth == 1, "single-layer trunk"
    batch, seq, _ = x_prompt.shape
    dbatch, dseq, _ = x_sample.shape
    dn_alpha = (2.0 * depth) ** 0.25
    lam_init = 0.8 - 0.6 * math.exp(-0.3 * 0)
    n_p, n_s = batch * seq, dbatch * dseq

    w_in_b = w_in[0].astype(BF16)
    wr_b, wd_b, wo_b = w_branch_ret[0].astype(BF16), w_branch_diff[0].astype(BF16), w_out[0].astype(BF16)
    w_rt_b = w_router[0].astype(BF16)
    w_gu_b, w_dn_b = w_gate_up[0].astype(BF16), w_down[0].astype(BF16)
    lam_args = (lam_q1, lam_k1, lam_q2, lam_k2, subln_w)

    xp = x_prompt.reshape(n_p, D_MODEL)
    qk, vr, gr, qd, kd_t, vd, ga, gb = _project_all(xp, w_in_b, jnp.arange(seq), 1024, 512, (batch, seq))
    o_r, s_p = _retention_prompt(qk, vr, gr, batch, seq)
    o_d = _dattn_prompt(lam_args, qd, kd_t, vd, batch, seq, lam_init)
    x1_p, idx_p, gate_p = _mix(xp, o_r, o_d, ga, gb, wr_b, wd_b, wo_b, ln1_g, ln1_b, w_rt_b, b_router, 256, dn_alpha)

    xs_ = x_sample.reshape(n_s, D_MODEL)
    pos_s = jnp.tile(PAST_LEN + jnp.arange(dseq), dbatch)
    qk_s, vr_s, gr_s, qd_s, kd_s, vd_s, ga_s, gb_s = _project_all(xs_, w_in_b, pos_s, n_s, n_s)
    pad8 = lambda a: jnp.pad(a.reshape(dbatch, dseq, -1), ((0, 0), (0, 8 - dseq), (0, 0)))
    pad_page = lambda a: jnp.pad(a.reshape(dbatch, dseq, -1), ((0, 0), (0, PAGE_SIZE - dseq), (0, 0)))
    o_r_s, s_s = _retention_sample(pad8(qk_s), pad8(vr_s), pad8(gr_s), state_ret[0], dseq)
    n_pool = cache_k.shape[1]
    cache_kt = jnp.transpose(cache_k[0], (0, 2, 3, 4, 1)).reshape(n_pool, D_HEADS * 2 * D_DH, PAGE_SIZE)
    cache_vr = cache_v[0].reshape(n_pool, PAGE_SIZE * D_HEADS, D_DV)
    kn_t = jnp.transpose(pad_page(kd_s), (0, 2, 1))
    vn = pad_page(vd_s).reshape(dbatch, PAGE_SIZE * D_HEADS, D_DV)
    o_d_s = _dattn_decode(lam_args, page_table, pad8(qd_s), kn_t, vn, cache_kt, cache_vr, lam_init)
    o_r_s = o_r_s[:, :dseq].reshape(n_s, -1)
    o_d_s = o_d_s[:, :dseq].reshape(n_s, -1)
    x1_s, idx_s, gate_s = _mix(xs_, o_r_s, o_d_s, ga_s, gb_s, wr_b, wd_b, wo_b, ln1_g, ln1_b, w_rt_b, b_router,
                               n_s, dn_alpha)

    y_p, y_s = _moe([(x1_p, idx_p, gate_p), (x1_s, idx_s, gate_s)],
                    w_gu_b, b_gate_up[0][:, None, :], w_dn_b, b_down[0][:, None, :], ln2_g, ln2_b, dn_alpha)

    return (y_p.reshape(batch, seq, D_MODEL),
            y_s.reshape(dbatch, dseq, D_MODEL),
            jnp.transpose(kd_t.reshape(batch, D_HEADS, 2, D_DH, seq), (0, 4, 1, 2, 3))[None],
            vd.reshape(1, batch, seq, D_HEADS, D_DV),
            s_p[None],
            kd_s.reshape(1, dbatch, dseq, D_HEADS, 2, D_DH),
            vd_s.reshape(1, dbatch, dseq, D_HEADS, D_DV),
            s_s[None])
```

```python
import functools
import math

import jax
import jax.numpy as jnp
from jax import lax
from jax.experimental import pallas as pl
from jax.experimental.pallas import tpu as pltpu

F32 = jnp.float32
BF16 = jnp.bfloat16

D_MODEL = 1024
PAST_LEN = 16384
PAGE_SIZE = 128
R_HEADS, R_DK, R_DV, R_CHUNK = 4, 128, 256, 128
D_HEADS, D_DH, D_DV = 8, 64, 128
ROPE_THETA = 10000.0
N_EXPERTS, TOP_K, D_FF = 32, 4, 1024
SWIGLU_ALPHA, SWIGLU_LIMIT = 1.702, 7.0
NEG_INF = -1e30

LANE = 128
VMEM_LIMIT = 56 * 1024 * 1024
ATTN_BLOCK = 512
DEC_PAGES = 8
MOE_TM = 512
TOK_BLOCK = 128
COL = 1024


def _cparams(*sem):
    return pltpu.CompilerParams(dimension_semantics=sem, vmem_limit_bytes=VMEM_LIMIT)


def _dot(a, b):
    return jnp.dot(a, b, preferred_element_type=F32)


def _dot_nt(a, b):
    return lax.dot_general(a, b, (((1,), (1,)), ((), ())), preferred_element_type=F32)


def _swap_halves(h, unit):
    outs = []
    lane = lax.broadcasted_iota(jnp.int32, (h.shape[0], LANE), 1)
    for g in range(h.shape[1] // LANE):
        sl = h[:, g * LANE:(g + 1) * LANE]
        if unit == LANE:
            outs.append(pltpu.roll(sl, LANE // 2, 1))
        else:
            fwd = pltpu.roll(sl, unit // 2, 1)
            bwd = pltpu.roll(sl, LANE - unit // 2, 1)
            outs.append(jnp.where((lane % unit) < unit // 2, bwd, fwd))
    return jnp.concatenate(outs, axis=1)


def _proj_kernel(x_ref, w_ref, o_ref):
    o_ref[...] = _dot(x_ref[...].astype(BF16), w_ref[...])


def _proj_rope_kernel(x_ref, w_ref, cos_ref, sin_ref, o_ref, *, unit):
    h = _dot(x_ref[...].astype(BF16), w_ref[...])
    o_ref[...] = h * cos_ref[...] + _swap_halves(h, unit) * sin_ref[...]


def _proj_rope_t_kernel(x_ref, w_ref, cos_ref, sin_ref, o_ref):
    ht = _dot(x_ref[...].astype(BF16), w_ref[...]).T
    half = D_DH // 2
    cos, sin = cos_ref[...], sin_ref[...]
    parts = []
    for u in range(ht.shape[0] // D_DH):
        blk = ht[u * D_DH:(u + 1) * D_DH]
        swapped = jnp.concatenate([blk[half:], blk[:half]], axis=0)
        parts.append(blk * cos + swapped * sin)
    o_ref[...] = jnp.concatenate(parts, axis=0)


def _proj_keys_t(x, w, col0, tm, cos_t, sin_t, batch, seq):
    n_pos = seq // tm
    return pl.pallas_call(
        _proj_rope_t_kernel, grid=(batch * n_pos,),
        in_specs=[pl.BlockSpec((tm, D_MODEL), lambda i: (i, 0)),
                  pl.BlockSpec((D_MODEL, COL), lambda i: (0, col0 // COL)),
                  pl.BlockSpec((D_DH, tm), lambda i: (0, i % n_pos)),
                  pl.BlockSpec((D_DH, tm), lambda i: (0, i % n_pos))],
        out_specs=pl.BlockSpec((COL, tm), lambda i: (i // n_pos, i % n_pos)),
        out_shape=jax.ShapeDtypeStruct((batch * COL, seq), F32),
        compiler_params=_cparams("parallel"), name="proj_keys_t",
    )(x, w, cos_t, sin_t)


def _proj(x, w, col0, ncols, tm, rope=None):
    t = x.shape[0]
    grid = (t // tm, ncols // COL)
    in_specs = [pl.BlockSpec((tm, D_MODEL), lambda i, j: (i, 0)),
                pl.BlockSpec((D_MODEL, COL), lambda i, j: (0, col0 // COL + j))]
    args = [x, w]
    if rope is None:
        body = _proj_kernel
    else:
        cos, sin, unit = rope
        n_pos = cos.shape[0] // tm
        in_specs += [pl.BlockSpec((tm, COL), lambda i, j: (i % n_pos, j))] * 2
        args += [cos, sin]
        body = functools.partial(_proj_rope_kernel, unit=unit)
    return pl.pallas_call(
        body, grid=grid, in_specs=in_specs,
        out_specs=pl.BlockSpec((tm, COL), lambda i, j: (i, j)),
        out_shape=jax.ShapeDtypeStruct((t, ncols), F32),
        compiler_params=_cparams("parallel", "arbitrary"), name="proj" if rope is None else "proj_rope",
    )(*args)


def _rope_table(pos, unit, n_units, scale=1.0):
    half = unit // 2
    inv = ROPE_THETA ** (-jnp.arange(half, dtype=F32) / half)
    ang = pos.astype(F32)[:, None] * inv[None, :]
    cos, sin = jnp.cos(ang), jnp.sin(ang)
    c = jnp.concatenate([cos, cos], axis=-1) * scale
    s = jnp.concatenate([-sin, sin], axis=-1) * scale
    return jnp.tile(c, (1, n_units)), jnp.tile(s, (1, n_units))


def _ret_tables(chunk, n_tok):
    log_gamma = jnp.log(1.0 - 2.0 ** (-5.0 - jnp.arange(R_HEADS, dtype=F32)))
    idx = jnp.arange(chunk, dtype=F32)
    diff = idx[:, None] - idx[None, :]
    causal = diff >= 0
    dec = jnp.where(causal[None], jnp.exp(log_gamma[:, None, None] * jnp.where(causal, diff, 0.0)[None]), 0.0)
    qdec = jnp.exp(log_gamma[:, None] * (idx[None, :] + 1.0))
    kdec = jnp.exp(log_gamma[:, None] * (n_tok - 1.0 - idx[None, :]))
    sdec = jnp.exp(log_gamma * n_tok)
    return (dec,
            jnp.broadcast_to(qdec[:, :, None], (R_HEADS, chunk, R_DV)),
            jnp.broadcast_to(kdec[:, :, None], (R_HEADS, chunk, R_DK)),
            jnp.broadcast_to(sdec[:, None, None], (R_HEADS, 1, R_DV)))


def _ret_chunk(q, k, v, g, state, dec, qdec, kdec, sdec):
    qb, kb, vb = q.astype(BF16), k.astype(BF16), v.astype(BF16)
    scores = _dot_nt(qb, kb) * dec
    inner = _dot(scores.astype(BF16), vb)
    cross = _dot(qb, state.astype(BF16)) * qdec
    o = inner + cross
    kd_t = (k * kdec).T.astype(BF16)
    s_new = state * sdec + _dot(kd_t, vb)
    mu = jnp.mean(o, axis=-1, keepdims=True)
    var = jnp.mean(jnp.square(o - mu), axis=-1, keepdims=True)
    o = (o - mu) * lax.rsqrt(var + 1e-6)
    return o * (g * jax.nn.sigmoid(g)), s_new


def _ret_prompt_kernel(q_ref, k_ref, v_ref, g_ref, dec_ref, qdec_ref, kdec_ref, sdec_ref, o_ref, s_ref):
    @pl.when(pl.program_id(1) == 0)
    def _():
        s_ref[...] = jnp.zeros_like(s_ref)

    for h in range(R_HEADS):
        dk, dv = slice(h * R_DK, (h + 1) * R_DK), slice(h * R_DV, (h + 1) * R_DV)
        o, s_new = _ret_chunk(q_ref[:, dk], k_ref[:, dk], v_ref[:, dv], g_ref[:, dv], s_ref[0, h],
                              dec_ref[h], qdec_ref[h], kdec_ref[h], sdec_ref[h])
        o_ref[:, dv] = o
        s_ref[0, h] = s_new


def _retention_prompt(qk, vr, gr, batch, seq):
    nc = seq // R_CHUNK
    tabs = _ret_tables(R_CHUNK, R_CHUNK)
    row = lambda b, c: b * nc + c
    whole = lambda a: pl.BlockSpec(a.shape, lambda b, c: (0,) * a.ndim)
    return pl.pallas_call(
        _ret_prompt_kernel, grid=(batch, nc),
        in_specs=[
            pl.BlockSpec((R_CHUNK, R_HEADS * R_DK), lambda b, c: (row(b, c), 0)),
            pl.BlockSpec((R_CHUNK, R_HEADS * R_DK), lambda b, c: (row(b, c), 1)),
            pl.BlockSpec((R_CHUNK, R_HEADS * R_DV), lambda b, c: (row(b, c), 0)),
            pl.BlockSpec((R_CHUNK, R_HEADS * R_DV), lambda b, c: (row(b, c), 0)),
        ] + [whole(a) for a in tabs],
        out_specs=[
            pl.BlockSpec((R_CHUNK, R_HEADS * R_DV), lambda b, c: (row(b, c), 0)),
            pl.BlockSpec((1, R_HEADS, R_DK, R_DV), lambda b, c: (b, 0, 0, 0)),
        ],
        out_shape=[jax.ShapeDtypeStruct((batch * seq, R_HEADS * R_DV), F32),
                   jax.ShapeDtypeStruct((batch, R_HEADS, R_DK, R_DV), F32)],
        compiler_params=_cparams("parallel", "arbitrary"), name="retention_prompt",
    )(qk, qk, vr, gr, *tabs)


def _ret_sample_kernel(q_ref, k_ref, v_ref, g_ref, s0_ref, dec_ref, qdec_ref, kdec_ref, sdec_ref,
                       o_ref, s_ref, qp, kp, vp, gp):
    n = q_ref.shape[1]
    for pad, src in ((qp, q_ref), (kp, k_ref), (vp, v_ref), (gp, g_ref)):
        pad[...] = jnp.zeros_like(pad)
        pad[0:n, :] = src[0]
    o, s_new = _ret_chunk(qp[...], kp[...], vp[...], gp[...], s0_ref[0, 0],
                          dec_ref[0], qdec_ref[0], kdec_ref[0], sdec_ref[0])
    o_ref[0] = o[0:n, :]
    s_ref[0, 0] = s_new


def _retention_sample(qk, vr, gr, state, n_tok):
    batch, rows, _ = qk.shape
    tabs = _ret_tables(R_CHUNK, n_tok)
    return pl.pallas_call(
        _ret_sample_kernel, grid=(batch, R_HEADS),
        in_specs=[
            pl.BlockSpec((1, rows, R_DK), lambda b, h: (b, 0, h)),
            pl.BlockSpec((1, rows, R_DK), lambda b, h: (b, 0, R_HEADS + h)),
            pl.BlockSpec((1, rows, R_DV), lambda b, h: (b, 0, h)),
            pl.BlockSpec((1, rows, R_DV), lambda b, h: (b, 0, h)),
            pl.BlockSpec((1, 1, R_DK, R_DV), lambda b, h: (b, h, 0, 0)),
            pl.BlockSpec((1, R_CHUNK, R_CHUNK), lambda b, h: (h, 0, 0)),
            pl.BlockSpec((1, R_CHUNK, R_DV), lambda b, h: (h, 0, 0)),
            pl.BlockSpec((1, R_CHUNK, R_DK), lambda b, h: (h, 0, 0)),
            pl.BlockSpec((1, 1, R_DV), lambda b, h: (h, 0, 0)),
        ],
        out_specs=[
            pl.BlockSpec((1, rows, R_DV), lambda b, h: (b, 0, h)),
            pl.BlockSpec((1, 1, R_DK, R_DV), lambda b, h: (b, h, 0, 0)),
        ],
        out_shape=[jax.ShapeDtypeStruct((batch, rows, R_HEADS * R_DV), F32),
                   jax.ShapeDtypeStruct((batch, R_HEADS, R_DK, R_DV), F32)],
        scratch_shapes=[pltpu.VMEM((R_CHUNK, R_DK), F32), pltpu.VMEM((R_CHUNK, R_DK), F32),
                        pltpu.VMEM((R_CHUNK, R_DV), F32), pltpu.VMEM((R_CHUNK, R_DV), F32)],
        compiler_params=_cparams("parallel", "parallel"), name="retention_sample",
    )(qk, qk, vr, gr, state, *tabs)


def _lambda(lq1_ref, lk1_ref, lq2_ref, lk2_ref, lam_init):
    a = jnp.sum(lq1_ref[...] * lk1_ref[...], axis=-1, keepdims=True)
    b = jnp.sum(lq2_ref[...] * lk2_ref[...], axis=-1, keepdims=True)
    return jnp.exp(a) - jnp.exp(b) + lam_init


def _sub_rms(o, w, lam_init):
    ms = jnp.mean(jnp.square(o), axis=-1, keepdims=True)
    return o * lax.rsqrt(ms + 1e-5) * w * (1.0 - lam_init)


def _dattn_prompt_kernel(lq1_ref, lk1_ref, lq2_ref, lk2_ref, w_ref, q_ref, kt_ref, v_ref, o_ref,
                         kc_ref, vx_ref, *, blk, lam_init):
    kc_ref[...] = kt_ref[...].astype(BF16)
    vx_ref[:, 0:D_DV] = v_ref[...].astype(BF16)
    vx_ref[:, D_DV:2 * D_DV] = jnp.ones((vx_ref.shape[0], D_DV), BF16)
    row = lax.broadcasted_iota(jnp.int32, (blk, blk), 0)
    col = lax.broadcasted_iota(jnp.int32, (blk, blk), 1)
    lam = _lambda(lq1_ref, lk1_ref, lq2_ref, lk2_ref, lam_init)

    for qi in range(q_ref.shape[0] // blk):
        q = q_ref[qi * blk:(qi + 1) * blk, :] * (D_DH ** -0.5)
        outs = []
        for c in range(2):
            qc = q[:, c * D_DH:(c + 1) * D_DH].astype(BF16)
            m = jnp.full((blk, 1), NEG_INF, F32)
            acc = jnp.zeros((blk, 2 * D_DV), F32)
            for j in range(qi + 1):
                s = _dot(qc, kc_ref[c * D_DH:(c + 1) * D_DH, j * blk:(j + 1) * blk])
                if j == qi:
                    s = jnp.where(col <= row, s, NEG_INF)
                m_new = jnp.maximum(m, jnp.max(s, axis=-1, keepdims=True))
                p = jnp.exp(s - m_new)
                acc = acc * jnp.exp(m - m_new) + _dot(p.astype(BF16), vx_ref[j * blk:(j + 1) * blk, :])
                m = m_new
            outs.append(acc[:, 0:D_DV] / acc[:, D_DV:D_DV + 1])
        o_ref[qi * blk:(qi + 1) * blk, :] = _sub_rms(outs[0] - lam * outs[1], w_ref[...], lam_init)


def _lam_specs(n):
    zero = lambda *_: (0, 0)
    return [pl.BlockSpec((1, D_DH), zero)] * 4 + [pl.BlockSpec((1, D_DV), zero)]


def _dattn_prompt(lam_args, qd, kd_t, vd, batch, seq, lam_init):
    blk = min(ATTN_BLOCK, seq)
    return pl.pallas_call(
        functools.partial(_dattn_prompt_kernel, blk=blk, lam_init=lam_init),
        grid=(batch, D_HEADS),
        in_specs=_lam_specs(2) + [
            pl.BlockSpec((seq, 2 * D_DH), lambda b, h: (b, h)),
            pl.BlockSpec((2 * D_DH, seq), lambda b, h: (b * D_HEADS + h, 0)),
            pl.BlockSpec((seq, D_DV), lambda b, h: (b, h)),
        ],
        out_specs=pl.BlockSpec((seq, D_DV), lambda b, h: (b, h)),
        out_shape=jax.ShapeDtypeStruct((batch * seq, D_HEADS * D_DV), F32),
        scratch_shapes=[pltpu.VMEM((2 * D_DH, seq), BF16), pltpu.VMEM((seq, 2 * D_DV), BF16)],
        compiler_params=_cparams("parallel", "parallel"), name="dattn_prompt",
    )(*lam_args, qd, kd_t, vd)


DEC_ROWS = D_HEADS * 2 * 8


def _dattn_decode_kernel(pt_ref, lq1_ref, lk1_ref, lq2_ref, lk2_ref, w_ref, q_ref, kn_ref, vn_ref, *rest,
                         n_pages, lam_init):
    k_refs, v_refs = rest[:n_pages], rest[n_pages:2 * n_pages]
    o_ref, qbd_ref, m_ref, l_ref, acc_ref = rest[2 * n_pages:]
    j = pl.program_id(1)

    @pl.when(j == 0)
    def _():
        q8 = q_ref[0] * (D_DH ** -0.5)
        tiled = jnp.concatenate([q8] * (DEC_ROWS // 8), axis=0)
        r = lax.broadcasted_iota(jnp.int32, tiled.shape, 0)
        cidx = lax.broadcasted_iota(jnp.int32, tiled.shape, 1)
        qbd_ref[...] = jnp.where(r // 8 == cidx // D_DH, tiled, 0.0).astype(BF16)
        m_ref[...] = jnp.full(m_ref.shape, NEG_INF, F32)
        l_ref[...] = jnp.zeros(l_ref.shape, F32)
        acc_ref[...] = jnp.zeros(acc_ref.shape, F32)

    rows_h = DEC_ROWS // D_HEADS

    def update(s_list, v_pages):
        m = m_ref[...]
        m_new = m
        for s in s_list:
            m_new = jnp.maximum(m_new, jnp.max(s, axis=-1, keepdims=True))
        alpha = jnp.exp(m - m_new)
        p = jnp.concatenate([jnp.exp(s - m_new) for s in s_list], axis=1)
        l_ref[...] = l_ref[...] * alpha + jnp.sum(p, axis=-1, keepdims=True)
        m_ref[...] = m_new
        p = p.astype(BF16)
        for h in range(D_HEADS):
            rows = slice(h * rows_h, (h + 1) * rows_h)
            v_h = jnp.concatenate([v[0, pl.ds(h, PAGE_SIZE, stride=D_HEADS), :] for v in v_pages], axis=0)
            acc_ref[rows, :] = acc_ref[rows, :] * alpha[rows, :] + _dot(p[rows, :], v_h.astype(BF16))

    qbd = qbd_ref[...]
    update([_dot(qbd, k[0].astype(BF16)) for k in k_refs], v_refs)

    @pl.when(j == pl.num_programs(1) - 1)
    def _():
        s = _dot(qbd, kn_ref[0].astype(BF16))
        t_q = lax.broadcasted_iota(jnp.int32, s.shape, 0) % 8
        t_k = lax.broadcasted_iota(jnp.int32, s.shape, 1)
        update([jnp.where(t_k <= t_q, s, NEG_INF)], [vn_ref])
        acc = acc_ref[...] / l_ref[...]
        lam = _lambda(lq1_ref, lk1_ref, lq2_ref, lk2_ref, lam_init)
        w = w_ref[...]
        heads = []
        for h in range(D_HEADS):
            o = acc[h * rows_h:h * rows_h + 8, :] - lam * acc[h * rows_h + 8:(h + 1) * rows_h, :]
            heads.append(_sub_rms(o, w, lam_init))
        o_ref[0] = jnp.concatenate(heads, axis=1)


def _dattn_decode(lam_args, page_table, q8, kn_t, vn, cache_kt, cache_v, lam_init):
    batch, n_pages = page_table.shape
    steps = n_pages // DEC_PAGES
    width = q8.shape[-1]
    page = (1,) + cache_kt.shape[1:]

    def page_spec(p):
        return pl.BlockSpec(page, lambda b, j, pt: (pt[b * n_pages + j * DEC_PAGES + p], 0, 0))

    zero = lambda b, j, pt: (0, 0)
    seq_spec = lambda shape: pl.BlockSpec(shape, lambda b, j, pt: (b, 0, 0))
    grid_spec = pltpu.PrefetchScalarGridSpec(
        num_scalar_prefetch=1, grid=(batch, steps),
        in_specs=([pl.BlockSpec((1, D_DH), zero)] * 4 + [pl.BlockSpec((1, D_DV), zero)]
                  + [seq_spec((1, 8, width)), seq_spec(page), seq_spec(page)]
                  + [page_spec(p) for p in range(DEC_PAGES)] * 2),
        out_specs=seq_spec((1, 8, width)),
        scratch_shapes=[pltpu.VMEM((DEC_ROWS, width), BF16), pltpu.VMEM((DEC_ROWS, 1), F32),
                        pltpu.VMEM((DEC_ROWS, 1), F32), pltpu.VMEM((DEC_ROWS, D_DV), F32)])
    return pl.pallas_call(
        functools.partial(_dattn_decode_kernel, n_pages=DEC_PAGES, lam_init=lam_init),
        grid_spec=grid_spec,
        out_shape=jax.ShapeDtypeStruct((batch, 8, width), F32),
        compiler_params=_cparams("parallel", "arbitrary"), name="dattn_decode",
    )(page_table.reshape(-1), *lam_args, q8, kn_t, vn, *([cache_kt] * DEC_PAGES), *([cache_v] * DEC_PAGES))


def _layer_norm(y, g, b):
    mu = jnp.mean(y, axis=-1, keepdims=True)
    var = jnp.mean(jnp.square(y - mu), axis=-1, keepdims=True)
    return (y - mu) * lax.rsqrt(var + 1e-5) * g + b


def _mix_kernel(x_ref, or_ref, od_ref, ga_ref, gb_ref, wr_ref, wd_ref, wo_ref, g_ref, b_ref, wrt_ref, brt_ref,
                x1_ref, idx_ref, gate_ref, *, dn_alpha):
    r = _dot(or_ref[...].astype(BF16), wr_ref[...])
    d = _dot(od_ref[...].astype(BF16), wd_ref[...])
    mix = jax.nn.sigmoid(ga_ref[...]) * r + jax.nn.sigmoid(gb_ref[...]) * d
    y = dn_alpha * x_ref[...] + _dot(mix.astype(BF16), wo_ref[...])
    x1 = _layer_norm(y, g_ref[...], b_ref[...])
    x1_ref[...] = x1
    vals = _dot(x1.astype(BF16), wrt_ref[...]) + brt_ref[...]
    col = lax.broadcasted_iota(jnp.int32, vals.shape, 1)
    lane = lax.broadcasted_iota(jnp.int32, idx_ref.shape, 1)
    idx_out = jnp.zeros(idx_ref.shape, jnp.int32)
    val_out = jnp.full(gate_ref.shape, NEG_INF, F32)
    for k in range(TOP_K):
        mx = jnp.max(vals, axis=-1, keepdims=True)
        first = jnp.min(jnp.where(vals == mx, col, N_EXPERTS), axis=-1, keepdims=True)
        idx_out = jnp.where(lane == k, first, idx_out)
        val_out = jnp.where(lane == k, mx, val_out)
        vals = jnp.where(col == first, NEG_INF, vals)
    e = jnp.exp(val_out - jnp.max(val_out, axis=-1, keepdims=True))
    idx_ref[...] = idx_out
    gate_ref[...] = e / jnp.sum(e, axis=-1, keepdims=True)


def _mix(x, o_r, o_d, ga, gb, wr, wd, wo, g, b, w_rt, b_rt, tm, dn_alpha):
    t = x.shape[0]
    tok = pl.BlockSpec((tm, D_MODEL), lambda i: (i, 0))
    mat = pl.BlockSpec((D_MODEL, D_MODEL), lambda i: (0, 0))
    vec = pl.BlockSpec((1, D_MODEL), lambda i: (0, 0))
    return pl.pallas_call(
        functools.partial(_mix_kernel, dn_alpha=dn_alpha), grid=(t // tm,),
        in_specs=[tok] * 5 + [mat] * 3 + [vec, vec,
                                          pl.BlockSpec((D_MODEL, N_EXPERTS), lambda i: (0, 0)),
                                          pl.BlockSpec((1, N_EXPERTS), lambda i: (0, 0))],
        out_specs=[tok, pl.BlockSpec((tm, LANE), lambda i: (i, 0)), pl.BlockSpec((tm, LANE), lambda i: (i, 0))],
        out_shape=[jax.ShapeDtypeStruct((t, D_MODEL), F32), jax.ShapeDtypeStruct((t, LANE), jnp.int32),
                   jax.ShapeDtypeStruct((t, LANE), F32)],
        compiler_params=_cparams("parallel"), name="mix_ln_router",
    )(x, o_r, o_d, ga, gb, wr, wd, wo, g, b, w_rt, b_rt)


ROWS_PER_STEP = TOK_BLOCK * TOP_K


def _scatter_rows_kernel(dest_ref, x_ref, xs_in_ref, xs_ref, sems):
    del xs_in_ref
    i = pl.program_id(0)
    base = i * TOK_BLOCK

    def start(t, c):
        for k in range(TOP_K):
            pltpu.make_async_copy(x_ref.at[pl.ds(base + t, 1)],
                                  xs_ref.at[pl.ds(dest_ref[0, 0, t * TOP_K + k], 1)], sems.at[i % 2]).start()
        return c

    lax.fori_loop(0, TOK_BLOCK, start, 0, unroll=8)

    def drain(slot):
        pltpu.make_async_copy(xs_ref.at[pl.ds(0, ROWS_PER_STEP)], xs_ref.at[pl.ds(ROWS_PER_STEP, ROWS_PER_STEP)],
                              sems.at[slot]).wait()

    @pl.when(i > 0)
    def _():
        drain((i + 1) % 2)

    @pl.when(i == pl.num_programs(0) - 1)
    def _():
        drain(i % 2)


def _scatter_rows(dest3, x, xs):
    t = x.shape[0]
    return pl.pallas_call(
        _scatter_rows_kernel, grid=(t // TOK_BLOCK,),
        in_specs=[pl.BlockSpec((1, 1, ROWS_PER_STEP), lambda i: (i, 0, 0), memory_space=pltpu.SMEM),
                  pl.BlockSpec(memory_space=pl.ANY),
                  pl.BlockSpec(memory_space=pl.ANY)],
        out_specs=pl.BlockSpec(memory_space=pl.ANY),
        out_shape=jax.ShapeDtypeStruct(xs.shape, xs.dtype),
        scratch_shapes=[pltpu.SemaphoreType.DMA((2,))],
        input_output_aliases={2: 0},
        compiler_params=_cparams("arbitrary"), name="moe_scatter_rows",
    )(dest3, x, xs)


def _expert_kernel(blk_e_ref, n_used_ref, x_ref, wgu_ref, bgu_ref, wdn_ref, bdn_ref, y_ref):
    del blk_e_ref
    used = pl.program_id(0) < n_used_ref[0]

    @pl.when(jnp.logical_not(used))
    def _():
        y_ref[...] = jnp.zeros_like(y_ref)

    @pl.when(used)
    def _():
        hgu = _dot(x_ref[...].astype(BF16), wgu_ref[0]) + bgu_ref[0]
        glu = jnp.minimum(hgu[:, 0:D_FF], SWIGLU_LIMIT)
        lin = jnp.clip(hgu[:, D_FF:2 * D_FF], -SWIGLU_LIMIT, SWIGLU_LIMIT)
        act = glu * jax.nn.sigmoid(SWIGLU_ALPHA * glu) * (lin + 1.0)
        y_ref[...] = _dot(act.astype(BF16), wdn_ref[0]) + bdn_ref[0]


def _experts(blk_e, n_used, xs, w_gu, b_gu, w_dn, b_dn):
    n_rows = xs.shape[0]
    blk = lambda i, be, nu: jnp.minimum(i, nu[0] - 1)
    exp = lambda i, be, nu: be[jnp.minimum(i, nu[0] - 1)]
    grid_spec = pltpu.PrefetchScalarGridSpec(
        num_scalar_prefetch=2, grid=(n_rows // MOE_TM,),
        in_specs=[pl.BlockSpec((MOE_TM, D_MODEL), lambda i, be, nu: (blk(i, be, nu), 0)),
                  pl.BlockSpec((1, D_MODEL, 2 * D_FF), lambda i, be, nu: (exp(i, be, nu), 0, 0)),
                  pl.BlockSpec((1, 1, 2 * D_FF), lambda i, be, nu: (exp(i, be, nu), 0, 0)),
                  pl.BlockSpec((1, D_FF, D_MODEL), lambda i, be, nu: (exp(i, be, nu), 0, 0)),
                  pl.BlockSpec((1, 1, D_MODEL), lambda i, be, nu: (exp(i, be, nu), 0, 0))],
        out_specs=pl.BlockSpec((MOE_TM, D_MODEL), lambda i, be, nu: (i, 0)))
    return pl.pallas_call(
        _expert_kernel, grid_spec=grid_spec,
        out_shape=jax.ShapeDtypeStruct((n_rows, D_MODEL), F32),
        compiler_params=_cparams("arbitrary"), name="moe_experts",
    )(blk_e, n_used, xs, w_gu, b_gu, w_dn, b_dn)


def _combine_kernel(dest_ref, next_ref, gate_ref, x_ref, g_ref, b_ref, ys_ref, o_ref, buf, sems, *, dn_alpha):
    i = pl.program_id(0)
    slot = i % 2

    def issue(d_ref, s):
        def start(t, c):
            for k in range(TOP_K):
                pltpu.make_async_copy(ys_ref.at[pl.ds(d_ref[0, 0, t * TOP_K + k], 1)],
                                      buf.at[s, k, pl.ds(t, 1)], sems.at[s]).start()
            return c

        lax.fori_loop(0, TOK_BLOCK, start, 0, unroll=8)

    @pl.when(i == 0)
    def _():
        issue(dest_ref, 0)

    @pl.when(i + 1 < pl.num_programs(0))
    def _():
        issue(next_ref, 1 - slot)

    for k in range(TOP_K):
        pltpu.make_async_copy(ys_ref.at[pl.ds(0, TOK_BLOCK)], buf.at[slot, k], sems.at[slot]).wait()
    gate = gate_ref[...]
    y = dn_alpha * x_ref[...]
    for k in range(TOP_K):
        y = y + gate[:, k:k + 1] * buf[slot, k]
    o_ref[...] = _layer_norm(y, g_ref[...], b_ref[...])


def _combine(dest3, gates, x1, g, b, ys, dn_alpha):
    t = x1.shape[0]
    n = t // TOK_BLOCK
    tok = pl.BlockSpec((TOK_BLOCK, D_MODEL), lambda i: (i, 0))
    vec = pl.BlockSpec((1, D_MODEL), lambda i: (0, 0))
    return pl.pallas_call(
        functools.partial(_combine_kernel, dn_alpha=dn_alpha), grid=(n,),
        in_specs=[pl.BlockSpec((1, 1, ROWS_PER_STEP), lambda i: (i, 0, 0), memory_space=pltpu.SMEM),
                  pl.BlockSpec((1, 1, ROWS_PER_STEP), lambda i: (jnp.minimum(i + 1, n - 1), 0, 0),
                               memory_space=pltpu.SMEM),
                  pl.BlockSpec((TOK_BLOCK, LANE), lambda i: (i, 0)), tok, vec, vec,
                  pl.BlockSpec(memory_space=pl.ANY)],
        out_specs=tok,
        out_shape=jax.ShapeDtypeStruct((t, D_MODEL), F32),
        scratch_shapes=[pltpu.VMEM((2, TOP_K, TOK_BLOCK, D_MODEL), F32), pltpu.SemaphoreType.DMA((2,))],
        compiler_params=_cparams("arbitrary"), name="moe_combine_ln",
    )(dest3, dest3, gates, x1, g, b, ys)


def _onehots(idx):
    lane = lax.broadcasted_iota(jnp.int32, idx.shape, 1)
    return lane, [lane == idx[:, k:k + 1] for k in range(TOP_K)]


def _lanes(cols, lane):
    out = jnp.zeros(lane.shape, jnp.int32)
    for k, c in enumerate(cols):
        out = jnp.where(lane == k, c, out)
    return out


def _rank_kernel(cnt0_ref, idx_ref, rank_ref, cnt_ref, carry_ref):
    @pl.when(pl.program_id(0) == 0)
    def _():
        carry_ref[...] = cnt0_ref[...]

    n = TOK_BLOCK
    r = lax.broadcasted_iota(jnp.int32, (n, n), 0)
    c = lax.broadcasted_iota(jnp.int32, (n, n), 1)
    earlier = jnp.where(c < r, 1.0, 0.0).astype(BF16)
    carry = carry_ref[...]
    for sub in range(idx_ref.shape[0] // n):
        rows = slice(sub * n, (sub + 1) * n)
        lane, hots = _onehots(idx_ref[rows, :])
        chose = sum(h.astype(F32) for h in hots)
        before = _dot(earlier, chose.astype(BF16)) + carry
        ranks = [jnp.sum(jnp.where(h, before, 0.0), axis=-1, keepdims=True).astype(jnp.int32) for h in hots]
        rank_ref[rows, :] = _lanes(ranks, lane)
        carry = carry + jnp.sum(chose, axis=0, keepdims=True)
    carry_ref[...] = carry
    cnt_ref[...] = carry


def _route_block(t):
    return 1024 if t % 1024 == 0 else TOK_BLOCK


def _rank(cnt0, idx):
    t = idx.shape[0]
    tb = _route_block(t)
    tok = pl.BlockSpec((tb, LANE), lambda i: (i, 0))
    one = pl.BlockSpec((1, LANE), lambda i: (0, 0))
    return pl.pallas_call(
        _rank_kernel, grid=(t // tb,), in_specs=[one, tok], out_specs=[tok, one],
        out_shape=[jax.ShapeDtypeStruct((t, LANE), jnp.int32), jax.ShapeDtypeStruct((1, LANE), F32)],
        scratch_shapes=[pltpu.VMEM((1, LANE), F32)],
        compiler_params=_cparams("arbitrary"), name="moe_rank",
    )(cnt0, idx)


def _dest_kernel(cnt_ref, idx_ref, rank_ref, dest_ref, blk_ref, used_ref, start_ref):
    @pl.when(pl.program_id(0) == 0)
    def _():
        lane = lax.broadcasted_iota(jnp.int32, (8, LANE), 1)
        cnt = jnp.broadcast_to(cnt_ref[...], (8, LANE)).astype(jnp.int32)
        nblk = jnp.where(lane < N_EXPERTS, jnp.right_shift(cnt + (MOE_TM - 1), MOE_TM.bit_length() - 1), 0)
        end = nblk
        for s in (1, 2, 4, 8, 16):
            end = end + jnp.where(lane >= s, pltpu.roll(end, s, 1), 0)
        start_ref[...] = ((end - nblk) * MOE_TM).astype(F32)
        used_ref[...] = jnp.broadcast_to(end[:, N_EXPERTS - 1:N_EXPERTS], (8, LANE))
        blk = lax.broadcasted_iota(jnp.int32, blk_ref.shape, 0)
        lane_b = lax.broadcasted_iota(jnp.int32, blk_ref.shape, 1)
        done = jnp.where(lane_b < N_EXPERTS, jnp.where(end[0:1, :] <= blk, 1.0, 0.0), 0.0)
        blk_ref[...] = jnp.broadcast_to(
            jnp.minimum(jnp.sum(done, axis=-1, keepdims=True), N_EXPERTS - 1.0).astype(jnp.int32), blk_ref.shape)

    lane, hots = _onehots(idx_ref[...])
    start = start_ref[0:1, :]
    rank = rank_ref[...]
    dest_ref[...] = _lanes([jnp.sum(jnp.where(h, start, 0.0), axis=-1, keepdims=True).astype(jnp.int32)
                            + rank[:, k:k + 1] for k, h in enumerate(hots)], lane)


def _dest(cnt, idx, rank, n_blocks):
    t = idx.shape[0]
    tb = _route_block(t)
    tok = pl.BlockSpec((tb, LANE), lambda i: (i, 0))
    rows = -(-n_blocks // 8) * 8
    return pl.pallas_call(
        _dest_kernel, grid=(t // tb,),
        in_specs=[pl.BlockSpec((1, LANE), lambda i: (0, 0)), tok, tok],
        out_specs=[tok, pl.BlockSpec((rows, LANE), lambda i: (0, 0)), pl.BlockSpec((8, LANE), lambda i: (0, 0))],
        out_shape=[jax.ShapeDtypeStruct((t, LANE), jnp.int32), jax.ShapeDtypeStruct((rows, LANE), jnp.int32),
                   jax.ShapeDtypeStruct((8, LANE), jnp.int32)],
        scratch_shapes=[pltpu.VMEM((8, LANE), F32)],
        compiler_params=_cparams("arbitrary"), name="moe_dest",
    )(cnt, idx, rank)


def _moe(groups, w_gu, b_gu, w_dn, b_dn, g, b, dn_alpha):
    n_assign = sum(x1.shape[0] for x1, _, _ in groups) * TOP_K
    n_blocks = n_assign // MOE_TM + N_EXPERTS
    cnt = jnp.zeros((1, LANE), F32)
    ranks = []
    for _, idx, _ in groups:
        rank, cnt = _rank(cnt, idx)
        ranks.append(rank)
    xs = jnp.zeros((n_blocks * MOE_TM, D_MODEL), F32)
    dests = []
    for (x1, idx, _), rank in zip(groups, ranks):
        dest, blk, used = _dest(cnt, idx, rank, n_blocks)
        dest3 = dest[:, :TOP_K].reshape(-1, 1, TOK_BLOCK * TOP_K)
        xs = _scatter_rows(dest3, x1, xs)
        dests.append(dest3)
    ys = _experts(blk[:n_blocks, 0], used[0, :1], xs, w_gu, b_gu, w_dn, b_dn)
    return [_combine(dest3, gates, x1, g, b, ys, dn_alpha) for (x1, _, gates), dest3 in zip(groups, dests)]


def _project_all(x, w_in, pos, tm_plain, tm_rope, keys_t=None):
    ret_tab = [jnp.concatenate(p, axis=1) for p in zip(_rope_table(pos, R_DK, R_HEADS),
                                                       _rope_table(pos, R_DK, R_HEADS, R_DK ** -0.5))]
    diff_tab = _rope_table(pos, D_DH, 2 * D_HEADS)
    qk = _proj(x, w_in, 0, COL, tm_rope, (ret_tab[0], ret_tab[1], R_DK))
    vr = _proj(x, w_in, 1 * COL, COL, tm_plain)
    gr = _proj(x, w_in, 2 * COL, COL, tm_plain)
    qd = _proj(x, w_in, 3 * COL, COL, tm_rope, (diff_tab[0], diff_tab[1], D_DH))
    if keys_t is None:
        kd = _proj(x, w_in, 4 * COL, COL, tm_rope, (diff_tab[0], diff_tab[1], D_DH))
    else:
        cos_t, sin_t = (t[:, :D_DH].T for t in diff_tab)
        kd = _proj_keys_t(x, w_in, 4 * COL, tm_rope, cos_t, sin_t, *keys_t)
    vd = _proj(x, w_in, 5 * COL, COL, tm_plain)
    ga = _proj(x, w_in, 6 * COL, COL, tm_plain)
    gb = _proj(x, w_in, 7 * COL, COL, tm_plain)
    return qk, vr, gr, qd, kd, vd, ga, gb


def kernel(x_prompt, x_sample, cache_k, cache_v, state_ret, page_table, w_in, w_branch_ret, w_branch_diff, w_out, lam_q1, lam_k1, lam_q2, lam_k2, subln_w, ln1_g, ln1_b, w_router, b_router, w_gate_up, b_gate_up, w_down, b_down, ln2_g, ln2_b):
    depth = w_in.shape[0]
    assert depth == 1, "single-layer trunk"
    batch, seq, _ = x_prompt.shape
    dbatch, dseq, _ = x_sample.shape
    dn_alpha = (2.0 * depth) ** 0.25
    lam_init = 0.8 - 0.6 * math.exp(-0.3 * 0)
    n_p, n_s = batch * seq, dbatch * dseq

    w_in_b = w_in[0].astype(BF16)
    wr_b, wd_b, wo_b = w_branch_ret[0].astype(BF16), w_branch_diff[0].astype(BF16), w_out[0].astype(BF16)
    w_rt_b = w_router[0].astype(BF16)
    w_gu_b, w_dn_b = w_gate_up[0].astype(BF16), w_down[0].astype(BF16)
    lam_args = (lam_q1, lam_k1, lam_q2, lam_k2, subln_w)

    xp = x_prompt.reshape(n_p, D_MODEL)
    qk, vr, gr, qd, kd_t, vd, ga, gb = _project_all(xp, w_in_b, jnp.arange(seq), 1024, 512, (batch, seq))
    o_r, s_p = _retention_prompt(qk, vr, gr, batch, seq)
    o_d = _dattn_prompt(lam_args, qd, kd_t, vd, batch, seq, lam_init)
    x1_p, idx_p, gate_p = _mix(xp, o_r, o_d, ga, gb, wr_b, wd_b, wo_b, ln1_g, ln1_b, w_rt_b, b_router, 256, dn_alpha)

    xs_ = x_sample.reshape(n_s, D_MODEL)
    pos_s = jnp.tile(PAST_LEN + jnp.arange(dseq), dbatch)
    qk_s, vr_s, gr_s, qd_s, kd_s, vd_s, ga_s, gb_s = _project_all(xs_, w_in_b, pos_s, n_s, n_s)
    pad8 = lambda a: jnp.pad(a.reshape(dbatch, dseq, -1), ((0, 0), (0, 8 - dseq), (0, 0)))
    pad_page = lambda a: jnp.pad(a.reshape(dbatch, dseq, -1), ((0, 0), (0, PAGE_SIZE - dseq), (0, 0)))
    o_r_s, s_s = _retention_sample(pad8(qk_s), pad8(vr_s), pad8(gr_s), state_ret[0], dseq)
    n_pool = cache_k.shape[1]
    cache_kt = jnp.transpose(cache_k[0], (0, 2, 3, 4, 1)).reshape(n_pool, D_HEADS * 2 * D_DH, PAGE_SIZE)
    cache_vr = cache_v[0].reshape(n_pool, PAGE_SIZE * D_HEADS, D_DV)
    kn_t = jnp.transpose(pad_page(kd_s), (0, 2, 1))
    vn = pad_page(vd_s).reshape(dbatch, PAGE_SIZE * D_HEADS, D_DV)
    o_d_s = _dattn_decode(lam_args, page_table, pad8(qd_s), kn_t, vn, cache_kt, cache_vr, lam_init)
    o_r_s = o_r_s[:, :dseq].reshape(n_s, -1)
    o_d_s = o_d_s[:, :dseq].reshape(n_s, -1)
    x1_s, idx_s, gate_s = _mix(xs_, o_r_s, o_d_s, ga_s, gb_s, wr_b, wd_b, wo_b, ln1_g, ln1_b, w_rt_b, b_router,
                               n_s, dn_alpha)

    y_p, y_s = _moe([(x1_p, idx_p, gate_p), (x1_s, idx_s, gate_s)],
                    w_gu_b, b_gate_up[0][:, None, :], w_dn_b, b_down[0][:, None, :], ln2_g, ln2_b, dn_alpha)

    return (y_p.reshape(batch, seq, D_MODEL),
            y_s.reshape(dbatch, dseq, D_MODEL),
            jnp.transpose(kd_t.reshape(batch, D_HEADS, 2, D_DH, seq), (0, 4, 1, 2, 3))[None],
            vd.reshape(1, batch, seq, D_HEADS, D_DV),
            s_p[None],
            kd_s.reshape(1, dbatch, dseq, D_HEADS, 2, D_DH),
            vd_s.reshape(1, dbatch, dseq, D_HEADS, D_DV),
            s_s[None])
```

```python
import functools
import math

import jax
import jax.numpy as jnp
from jax import lax
from jax.experimental import pallas as pl
from jax.experimental.pallas import tpu as pltpu

F32 = jnp.float32
BF16 = jnp.bfloat16

D_MODEL = 1024
PAST_LEN = 16384
PAGE_SIZE = 128
R_HEADS, R_DK, R_DV, R_CHUNK = 4, 128, 256, 128
D_HEADS, D_DH, D_DV = 8, 64, 128
ROPE_THETA = 10000.0
N_EXPERTS, TOP_K, D_FF = 32, 4, 1024
SWIGLU_ALPHA, SWIGLU_LIMIT = 1.702, 7.0
NEG_INF = -1e30

LANE = 128
VMEM_LIMIT = 56 * 1024 * 1024
ATTN_BLOCK = 512
DEC_PAGES = 8
MOE_TM = 512
TOK_BLOCK = 128
COL = 1024


def _cparams(*sem):
    return pltpu.CompilerParams(dimension_semantics=sem, vmem_limit_bytes=VMEM_LIMIT)


def _dot(a, b):
    return jnp.dot(a, b, preferred_element_type=F32)


def _dot_nt(a, b):
    return lax.dot_general(a, b, (((1,), (1,)), ((), ())), preferred_element_type=F32)


def _swap_halves(h, unit):
    outs = []
    lane = lax.broadcasted_iota(jnp.int32, (h.shape[0], LANE), 1)
    for g in range(h.shape[1] // LANE):
        sl = h[:, g * LANE:(g + 1) * LANE]
        if unit == LANE:
            outs.append(pltpu.roll(sl, LANE // 2, 1))
        else:
            fwd = pltpu.roll(sl, unit // 2, 1)
            bwd = pltpu.roll(sl, LANE - unit // 2, 1)
            outs.append(jnp.where((lane % unit) < unit // 2, bwd, fwd))
    return jnp.concatenate(outs, axis=1)


def _proj_kernel(x_ref, w_ref, o_ref):
    o_ref[...] = _dot(x_ref[...].astype(BF16), w_ref[...]).astype(o_ref.dtype)


def _proj_rope_kernel(x_ref, w_ref, cos_ref, sin_ref, o_ref, *, unit):
    h = _dot(x_ref[...].astype(BF16), w_ref[...])
    o_ref[...] = (h * cos_ref[...] + _swap_halves(h, unit) * sin_ref[...]).astype(o_ref.dtype)


def _proj_rope_t_kernel(x_ref, w_ref, cos_ref, sin_ref, o_ref):
    ht = _dot(x_ref[...].astype(BF16), w_ref[...]).T
    half = D_DH // 2
    cos, sin = cos_ref[...], sin_ref[...]
    parts = []
    for u in range(ht.shape[0] // D_DH):
        blk = ht[u * D_DH:(u + 1) * D_DH]
        swapped = jnp.concatenate([blk[half:], blk[:half]], axis=0)
        parts.append(blk * cos + swapped * sin)
    o_ref[...] = jnp.concatenate(parts, axis=0)


def _proj_keys_t(x, w, col0, tm, cos_t, sin_t, batch, seq):
    n_pos = seq // tm
    return pl.pallas_call(
        _proj_rope_t_kernel, grid=(batch * n_pos,),
        in_specs=[pl.BlockSpec((tm, D_MODEL), lambda i: (i, 0)),
                  pl.BlockSpec((D_MODEL, COL), lambda i: (0, col0 // COL)),
                  pl.BlockSpec((D_DH, tm), lambda i: (0, i % n_pos)),
                  pl.BlockSpec((D_DH, tm), lambda i: (0, i % n_pos))],
        out_specs=pl.BlockSpec((COL, tm), lambda i: (i // n_pos, i % n_pos)),
        out_shape=jax.ShapeDtypeStruct((batch * COL, seq), F32),
        compiler_params=_cparams("parallel"), name="proj_keys_t",
    )(x, w, cos_t, sin_t)


def _proj(x, w, col0, ncols, tm, rope=None, out_dtype=F32):
    t = x.shape[0]
    grid = (t // tm, ncols // COL)
    in_specs = [pl.BlockSpec((tm, D_MODEL), lambda i, j: (i, 0)),
                pl.BlockSpec((D_MODEL, COL), lambda i, j: (0, col0 // COL + j))]
    args = [x, w]
    if rope is None:
        body = _proj_kernel
    else:
        cos, sin, unit = rope
        n_pos = cos.shape[0] // tm
        in_specs += [pl.BlockSpec((tm, COL), lambda i, j: (i % n_pos, j))] * 2
        args += [cos, sin]
        body = functools.partial(_proj_rope_kernel, unit=unit)
    return pl.pallas_call(
        body, grid=grid, in_specs=in_specs,
        out_specs=pl.BlockSpec((tm, COL), lambda i, j: (i, j)),
        out_shape=jax.ShapeDtypeStruct((t, ncols), out_dtype),
        compiler_params=_cparams("parallel", "arbitrary"), name="proj" if rope is None else "proj_rope",
    )(*args)


def _rope_table(pos, unit, n_units, scale=1.0):
    half = unit // 2
    inv = ROPE_THETA ** (-jnp.arange(half, dtype=F32) / half)
    ang = pos.astype(F32)[:, None] * inv[None, :]
    cos, sin = jnp.cos(ang), jnp.sin(ang)
    c = jnp.concatenate([cos, cos], axis=-1) * scale
    s = jnp.concatenate([-sin, sin], axis=-1) * scale
    return jnp.tile(c, (1, n_units)), jnp.tile(s, (1, n_units))


def _ret_tables(chunk, n_tok):
    log_gamma = jnp.log(1.0 - 2.0 ** (-5.0 - jnp.arange(R_HEADS, dtype=F32)))
    idx = jnp.arange(chunk, dtype=F32)
    diff = idx[:, None] - idx[None, :]
    causal = diff >= 0
    dec = jnp.where(causal[None], jnp.exp(log_gamma[:, None, None] * jnp.where(causal, diff, 0.0)[None]), 0.0)
    qdec = jnp.exp(log_gamma[:, None] * (idx[None, :] + 1.0))
    kdec = jnp.exp(log_gamma[:, None] * (n_tok - 1.0 - idx[None, :]))
    sdec = jnp.exp(log_gamma * n_tok)
    return (dec,
            jnp.broadcast_to(qdec[:, :, None], (R_HEADS, chunk, R_DV)),
            jnp.broadcast_to(kdec[:, :, None], (R_HEADS, chunk, R_DK)),
            jnp.broadcast_to(sdec[:, None, None], (R_HEADS, 1, R_DV)))


def _ret_chunk(q, k, v, g, state, dec, qdec, kdec, sdec):
    qb, kb, vb = q.astype(BF16), k.astype(BF16), v.astype(BF16)
    scores = _dot_nt(qb, kb) * dec
    inner = _dot(scores.astype(BF16), vb)
    cross = _dot(qb, state.astype(BF16)) * qdec
    o = inner + cross
    kd_t = (k * kdec).T.astype(BF16)
    s_new = state * sdec + _dot(kd_t, vb)
    mu = jnp.mean(o, axis=-1, keepdims=True)
    var = jnp.mean(jnp.square(o - mu), axis=-1, keepdims=True)
    o = (o - mu) * lax.rsqrt(var + 1e-6)
    return o * (g * jax.nn.sigmoid(g)), s_new


def _ret_prompt_kernel(q_ref, k_ref, v_ref, g_ref, dec_ref, qdec_ref, kdec_ref, sdec_ref, o_ref, s_ref):
    @pl.when(pl.program_id(1) == 0)
    def _():
        s_ref[...] = jnp.zeros_like(s_ref)

    for h in range(R_HEADS):
        dk, dv = slice(h * R_DK, (h + 1) * R_DK), slice(h * R_DV, (h + 1) * R_DV)
        o, s_new = _ret_chunk(q_ref[:, dk], k_ref[:, dk], v_ref[:, dv], g_ref[:, dv], s_ref[0, h],
                              dec_ref[h], qdec_ref[h], kdec_ref[h], sdec_ref[h])
        o_ref[:, dv] = o.astype(o_ref.dtype)
        s_ref[0, h] = s_new


def _retention_prompt(qk, vr, gr, batch, seq):
    nc = seq // R_CHUNK
    tabs = _ret_tables(R_CHUNK, R_CHUNK)
    row = lambda b, c: b * nc + c
    whole = lambda a: pl.BlockSpec(a.shape, lambda b, c: (0,) * a.ndim)
    return pl.pallas_call(
        _ret_prompt_kernel, grid=(batch, nc),
        in_specs=[
            pl.BlockSpec((R_CHUNK, R_HEADS * R_DK), lambda b, c: (row(b, c), 0)),
            pl.BlockSpec((R_CHUNK, R_HEADS * R_DK), lambda b, c: (row(b, c), 1)),
            pl.BlockSpec((R_CHUNK, R_HEADS * R_DV), lambda b, c: (row(b, c), 0)),
            pl.BlockSpec((R_CHUNK, R_HEADS * R_DV), lambda b, c: (row(b, c), 0)),
        ] + [whole(a) for a in tabs],
        out_specs=[
            pl.BlockSpec((R_CHUNK, R_HEADS * R_DV), lambda b, c: (row(b, c), 0)),
            pl.BlockSpec((1, R_HEADS, R_DK, R_DV), lambda b, c: (b, 0, 0, 0)),
        ],
        out_shape=[jax.ShapeDtypeStruct((batch * seq, R_HEADS * R_DV), BF16),
                   jax.ShapeDtypeStruct((batch, R_HEADS, R_DK, R_DV), F32)],
        compiler_params=_cparams("parallel", "arbitrary"), name="retention_prompt",
    )(qk, qk, vr, gr, *tabs)


def _ret_sample_kernel(q_ref, k_ref, v_ref, g_ref, s0_ref, dec_ref, qdec_ref, kdec_ref, sdec_ref,
                       o_ref, s_ref, qp, kp, vp, gp):
    n = q_ref.shape[1]
    for pad, src in ((qp, q_ref), (kp, k_ref), (vp, v_ref), (gp, g_ref)):
        pad[...] = jnp.zeros_like(pad)
        pad[0:n, :] = src[0]
    o, s_new = _ret_chunk(qp[...], kp[...], vp[...], gp[...], s0_ref[0, 0],
                          dec_ref[0], qdec_ref[0], kdec_ref[0], sdec_ref[0])
    o_ref[0] = o[0:n, :]
    s_ref[0, 0] = s_new


def _retention_sample(qk, vr, gr, state, n_tok):
    batch, rows, _ = qk.shape
    tabs = _ret_tables(R_CHUNK, n_tok)
    return pl.pallas_call(
        _ret_sample_kernel, grid=(batch, R_HEADS),
        in_specs=[
            pl.BlockSpec((1, rows, R_DK), lambda b, h: (b, 0, h)),
            pl.BlockSpec((1, rows, R_DK), lambda b, h: (b, 0, R_HEADS + h)),
            pl.BlockSpec((1, rows, R_DV), lambda b, h: (b, 0, h)),
            pl.BlockSpec((1, rows, R_DV), lambda b, h: (b, 0, h)),
            pl.BlockSpec((1, 1, R_DK, R_DV), lambda b, h: (b, h, 0, 0)),
            pl.BlockSpec((1, R_CHUNK, R_CHUNK), lambda b, h: (h, 0, 0)),
            pl.BlockSpec((1, R_CHUNK, R_DV), lambda b, h: (h, 0, 0)),
            pl.BlockSpec((1, R_CHUNK, R_DK), lambda b, h: (h, 0, 0)),
            pl.BlockSpec((1, 1, R_DV), lambda b, h: (h, 0, 0)),
        ],
        out_specs=[
            pl.BlockSpec((1, rows, R_DV), lambda b, h: (b, 0, h)),
            pl.BlockSpec((1, 1, R_DK, R_DV), lambda b, h: (b, h, 0, 0)),
        ],
        out_shape=[jax.ShapeDtypeStruct((batch, rows, R_HEADS * R_DV), F32),
                   jax.ShapeDtypeStruct((batch, R_HEADS, R_DK, R_DV), F32)],
        scratch_shapes=[pltpu.VMEM((R_CHUNK, R_DK), F32), pltpu.VMEM((R_CHUNK, R_DK), F32),
                        pltpu.VMEM((R_CHUNK, R_DV), F32), pltpu.VMEM((R_CHUNK, R_DV), F32)],
        compiler_params=_cparams("parallel", "parallel"), name="retention_sample",
    )(qk, qk, vr, gr, state, *tabs)


def _lambda(lq1_ref, lk1_ref, lq2_ref, lk2_ref, lam_init):
    a = jnp.sum(lq1_ref[...] * lk1_ref[...], axis=-1, keepdims=True)
    b = jnp.sum(lq2_ref[...] * lk2_ref[...], axis=-1, keepdims=True)
    return jnp.exp(a) - jnp.exp(b) + lam_init


def _sub_rms(o, w, lam_init):
    ms = jnp.mean(jnp.square(o), axis=-1, keepdims=True)
    return o * lax.rsqrt(ms + 1e-5) * w * (1.0 - lam_init)


def _dattn_prompt_kernel(lq1_ref, lk1_ref, lq2_ref, lk2_ref, w_ref, q_ref, kt_ref, v_ref, o_ref,
                         kc_ref, vx_ref, *, blk, lam_init):
    kc_ref[...] = kt_ref[...].astype(BF16)
    vx_ref[:, 0:D_DV] = v_ref[...].astype(BF16)
    vx_ref[:, D_DV:2 * D_DV] = jnp.ones((vx_ref.shape[0], D_DV), BF16)
    row = lax.broadcasted_iota(jnp.int32, (blk, blk), 0)
    col = lax.broadcasted_iota(jnp.int32, (blk, blk), 1)
    lam = _lambda(lq1_ref, lk1_ref, lq2_ref, lk2_ref, lam_init)

    for qi in range(q_ref.shape[0] // blk):
        q = q_ref[qi * blk:(qi + 1) * blk, :] * (D_DH ** -0.5)
        outs = []
        for c in range(2):
            qc = q[:, c * D_DH:(c + 1) * D_DH].astype(BF16)
            m = jnp.full((blk, 1), NEG_INF, F32)
            acc = jnp.zeros((blk, 2 * D_DV), F32)
            for j in range(qi + 1):
                s = _dot(qc, kc_ref[c * D_DH:(c + 1) * D_DH, j * blk:(j + 1) * blk])
                if j == qi:
                    s = jnp.where(col <= row, s, NEG_INF)
                m_new = jnp.maximum(m, jnp.max(s, axis=-1, keepdims=True))
                p = jnp.exp(s - m_new)
                acc = acc * jnp.exp(m - m_new) + _dot(p.astype(BF16), vx_ref[j * blk:(j + 1) * blk, :])
                m = m_new
            outs.append(acc[:, 0:D_DV] / acc[:, D_DV:D_DV + 1])
        o_ref[qi * blk:(qi + 1) * blk, :] = _sub_rms(outs[0] - lam * outs[1], w_ref[...],
                                                     lam_init).astype(o_ref.dtype)


def _lam_specs(n):
    zero = lambda *_: (0, 0)
    return [pl.BlockSpec((1, D_DH), zero)] * 4 + [pl.BlockSpec((1, D_DV), zero)]


def _dattn_prompt(lam_args, qd, kd_t, vd, batch, seq, lam_init):
    blk = min(ATTN_BLOCK, seq)
    return pl.pallas_call(
        functools.partial(_dattn_prompt_kernel, blk=blk, lam_init=lam_init),
        grid=(batch, D_HEADS),
        in_specs=_lam_specs(2) + [
            pl.BlockSpec((seq, 2 * D_DH), lambda b, h: (b, h)),
            pl.BlockSpec((2 * D_DH, seq), lambda b, h: (b * D_HEADS + h, 0)),
            pl.BlockSpec((seq, D_DV), lambda b, h: (b, h)),
        ],
        out_specs=pl.BlockSpec((seq, D_DV), lambda b, h: (b, h)),
        out_shape=jax.ShapeDtypeStruct((batch * seq, D_HEADS * D_DV), BF16),
        scratch_shapes=[pltpu.VMEM((2 * D_DH, seq), BF16), pltpu.VMEM((seq, 2 * D_DV), BF16)],
        compiler_params=_cparams("parallel", "parallel"), name="dattn_prompt",
    )(*lam_args, qd, kd_t, vd)


DEC_ROWS = D_HEADS * 2 * 8


def _dattn_decode_kernel(pt_ref, lq1_ref, lk1_ref, lq2_ref, lk2_ref, w_ref, q_ref, kn_ref, vn_ref, *rest,
                         n_pages, lam_init):
    k_refs, v_refs = rest[:n_pages], rest[n_pages:2 * n_pages]
    o_ref, qbd_ref, m_ref, l_ref, acc_ref = rest[2 * n_pages:]
    j = pl.program_id(1)

    @pl.when(j == 0)
    def _():
        q8 = q_ref[0] * (D_DH ** -0.5)
        tiled = jnp.concatenate([q8] * (DEC_ROWS // 8), axis=0)
        r = lax.broadcasted_iota(jnp.int32, tiled.shape, 0)
        cidx = lax.broadcasted_iota(jnp.int32, tiled.shape, 1)
        qbd_ref[...] = jnp.where(r // 8 == cidx // D_DH, tiled, 0.0).astype(BF16)
        m_ref[...] = jnp.full(m_ref.shape, NEG_INF, F32)
        l_ref[...] = jnp.zeros(l_ref.shape, F32)
        acc_ref[...] = jnp.zeros(acc_ref.shape, F32)

    rows_h = DEC_ROWS // D_HEADS

    def update(s_list, v_pages):
        m = m_ref[...]
        m_new = m
        for s in s_list:
            m_new = jnp.maximum(m_new, jnp.max(s, axis=-1, keepdims=True))
        alpha = jnp.exp(m - m_new)
        p = jnp.concatenate([jnp.exp(s - m_new) for s in s_list], axis=1)
        l_ref[...] = l_ref[...] * alpha + jnp.sum(p, axis=-1, keepdims=True)
        m_ref[...] = m_new
        p = p.astype(BF16)
        for h in range(D_HEADS):
            rows = slice(h * rows_h, (h + 1) * rows_h)
            v_h = jnp.concatenate([v[0, pl.ds(h, PAGE_SIZE, stride=D_HEADS), :] for v in v_pages], axis=0)
            acc_ref[rows, :] = acc_ref[rows, :] * alpha[rows, :] + _dot(p[rows, :], v_h.astype(BF16))

    qbd = qbd_ref[...]
    update([_dot(qbd, k[0].astype(BF16)) for k in k_refs], v_refs)

    @pl.when(j == pl.num_programs(1) - 1)
    def _():
        s = _dot(qbd, kn_ref[0].astype(BF16))
        t_q = lax.broadcasted_iota(jnp.int32, s.shape, 0) % 8
        t_k = lax.broadcasted_iota(jnp.int32, s.shape, 1)
        update([jnp.where(t_k <= t_q, s, NEG_INF)], [vn_ref])
        acc = acc_ref[...] / l_ref[...]
        lam = _lambda(lq1_ref, lk1_ref, lq2_ref, lk2_ref, lam_init)
        w = w_ref[...]
        heads = []
        for h in range(D_HEADS):
            o = acc[h * rows_h:h * rows_h + 8, :] - lam * acc[h * rows_h + 8:(h + 1) * rows_h, :]
            heads.append(_sub_rms(o, w, lam_init))
        o_ref[0] = jnp.concatenate(heads, axis=1)


def _dattn_decode(lam_args, page_table, q8, kn_t, vn, cache_kt, cache_v, lam_init):
    batch, n_pages = page_table.shape
    steps = n_pages // DEC_PAGES
    width = q8.shape[-1]
    page = (1,) + cache_kt.shape[1:]

    def page_spec(p):
        return pl.BlockSpec(page, lambda b, j, pt: (pt[b * n_pages + j * DEC_PAGES + p], 0, 0))

    zero = lambda b, j, pt: (0, 0)
    seq_spec = lambda shape: pl.BlockSpec(shape, lambda b, j, pt: (b, 0, 0))
    grid_spec = pltpu.PrefetchScalarGridSpec(
        num_scalar_prefetch=1, grid=(batch, steps),
        in_specs=([pl.BlockSpec((1, D_DH), zero)] * 4 + [pl.BlockSpec((1, D_DV), zero)]
                  + [seq_spec((1, 8, width)), seq_spec(page), seq_spec(page)]
                  + [page_spec(p) for p in range(DEC_PAGES)] * 2),
        out_specs=seq_spec((1, 8, width)),
        scratch_shapes=[pltpu.VMEM((DEC_ROWS, width), BF16), pltpu.VMEM((DEC_ROWS, 1), F32),
                        pltpu.VMEM((DEC_ROWS, 1), F32), pltpu.VMEM((DEC_ROWS, D_DV), F32)])
    return pl.pallas_call(
        functools.partial(_dattn_decode_kernel, n_pages=DEC_PAGES, lam_init=lam_init),
        grid_spec=grid_spec,
        out_shape=jax.ShapeDtypeStruct((batch, 8, width), F32),
        compiler_params=_cparams("parallel", "arbitrary"), name="dattn_decode",
    )(page_table.reshape(-1), *lam_args, q8, kn_t, vn, *([cache_kt] * DEC_PAGES), *([cache_v] * DEC_PAGES))


def _layer_norm(y, g, b):
    mu = jnp.mean(y, axis=-1, keepdims=True)
    var = jnp.mean(jnp.square(y - mu), axis=-1, keepdims=True)
    return (y - mu) * lax.rsqrt(var + 1e-5) * g + b


def _mix_kernel(x_ref, or_ref, od_ref, ga_ref, gb_ref, wr_ref, wd_ref, wo_ref, g_ref, b_ref, wrt_ref, brt_ref,
                x1_ref, idx_ref, gate_ref, *, dn_alpha):
    r = _dot(or_ref[...].astype(BF16), wr_ref[...])
    d = _dot(od_ref[...].astype(BF16), wd_ref[...])
    mix = jax.nn.sigmoid(ga_ref[...]) * r + jax.nn.sigmoid(gb_ref[...]) * d
    y = dn_alpha * x_ref[...] + _dot(mix.astype(BF16), wo_ref[...])
    x1 = _layer_norm(y, g_ref[...], b_ref[...])
    x1_ref[...] = x1
    vals = _dot(x1.astype(BF16), wrt_ref[...]) + brt_ref[...]
    col = lax.broadcasted_iota(jnp.int32, vals.shape, 1)
    lane = lax.broadcasted_iota(jnp.int32, idx_ref.shape, 1)
    idx_out = jnp.zeros(idx_ref.shape, jnp.int32)
    val_out = jnp.full(gate_ref.shape, NEG_INF, F32)
    for k in range(TOP_K):
        mx = jnp.max(vals, axis=-1, keepdims=True)
        first = jnp.min(jnp.where(vals == mx, col, N_EXPERTS), axis=-1, keepdims=True)
        idx_out = jnp.where(lane == k, first, idx_out)
        val_out = jnp.where(lane == k, mx, val_out)
        vals = jnp.where(col == first, NEG_INF, vals)
    e = jnp.exp(val_out - jnp.max(val_out, axis=-1, keepdims=True))
    idx_ref[...] = idx_out
    gate_ref[...] = e / jnp.sum(e, axis=-1, keepdims=True)


def _mix(x, o_r, o_d, ga, gb, wr, wd, wo, g, b, w_rt, b_rt, tm, dn_alpha):
    t = x.shape[0]
    tok = pl.BlockSpec((tm, D_MODEL), lambda i: (i, 0))
    mat = pl.BlockSpec((D_MODEL, D_MODEL), lambda i: (0, 0))
    vec = pl.BlockSpec((1, D_MODEL), lambda i: (0, 0))
    return pl.pallas_call(
        functools.partial(_mix_kernel, dn_alpha=dn_alpha), grid=(t // tm,),
        in_specs=[tok] * 5 + [mat] * 3 + [vec, vec,
                                          pl.BlockSpec((D_MODEL, N_EXPERTS), lambda i: (0, 0)),
                                          pl.BlockSpec((1, N_EXPERTS), lambda i: (0, 0))],
        out_specs=[tok, pl.BlockSpec((tm, LANE), lambda i: (i, 0)), pl.BlockSpec((tm, LANE), lambda i: (i, 0))],
        out_shape=[jax.ShapeDtypeStruct((t, D_MODEL), F32), jax.ShapeDtypeStruct((t, LANE), jnp.int32),
                   jax.ShapeDtypeStruct((t, LANE), F32)],
        compiler_params=_cparams("parallel"), name="mix_ln_router",
    )(x, o_r, o_d, ga, gb, wr, wd, wo, g, b, w_rt, b_rt)


ROWS_PER_STEP = TOK_BLOCK * TOP_K


SCATTER_SLOTS = 3


def _scatter_rows_kernel(dest_ref, x_ref, xs_in_ref, xs_ref, xbuf, load_sems, row_sems):
    del xs_in_ref
    i = pl.program_id(0)
    n = pl.num_programs(0)
    slot = i % SCATTER_SLOTS

    def load(step):
        s = step % SCATTER_SLOTS
        rows = pl.ds(pl.multiple_of(step * TOK_BLOCK, TOK_BLOCK), TOK_BLOCK)
        return pltpu.make_async_copy(x_ref.at[rows], xbuf.at[s], load_sems.at[s])

    @pl.when(i == 0)
    def _():
        load(0).start()

        @pl.when(n > 1)
        def _():
            load(1).start()

    load(i).wait()

    def start(t, c):
        for k in range(TOP_K):
            pltpu.make_async_copy(xbuf.at[slot, pl.ds(t, 1)], xs_ref.at[pl.ds(dest_ref[0, 0, t * TOP_K + k], 1)],
                                  row_sems.at[slot]).start(priority=k % 2)
        return c

    lax.fori_loop(0, TOK_BLOCK, start, 0, unroll=8)

    def drain(s):
        pltpu.make_async_copy(xs_ref.at[pl.ds(0, ROWS_PER_STEP)], xs_ref.at[pl.ds(ROWS_PER_STEP, ROWS_PER_STEP)],
                              row_sems.at[s]).wait()

    @pl.when(i > 0)
    def _():
        drain((i - 1) % SCATTER_SLOTS)

    @pl.when(i + 2 < n)
    def _():
        load(i + 2).start()

    @pl.when(i == n - 1)
    def _():
        drain(slot)


def _scatter_rows(dest3, x, xs):
    t = x.shape[0]
    return pl.pallas_call(
        _scatter_rows_kernel, grid=(t // TOK_BLOCK,),
        in_specs=[pl.BlockSpec((1, 1, ROWS_PER_STEP), lambda i: (i, 0, 0), memory_space=pltpu.SMEM),
                  pl.BlockSpec(memory_space=pl.ANY),
                  pl.BlockSpec(memory_space=pl.ANY)],
        out_specs=pl.BlockSpec(memory_space=pl.ANY),
        out_shape=jax.ShapeDtypeStruct(xs.shape, xs.dtype),
        scratch_shapes=[pltpu.VMEM((SCATTER_SLOTS, TOK_BLOCK, D_MODEL), F32),
                        pltpu.SemaphoreType.DMA((SCATTER_SLOTS,)), pltpu.SemaphoreType.DMA((SCATTER_SLOTS,))],
        input_output_aliases={2: 0},
        compiler_params=_cparams("arbitrary"), name="moe_scatter_rows",
    )(dest3, x, xs)


def _expert_kernel(blk_e_ref, n_used_ref, x_ref, wgu_ref, bgu_ref, wdn_ref, bdn_ref, y_ref, wgu_b, wdn_b):
    i = pl.program_id(0)
    used = i < n_used_ref[0]

    @pl.when(jnp.logical_not(used))
    def _():
        y_ref[...] = jnp.zeros_like(y_ref)

    first = jnp.logical_or(i == 0, blk_e_ref[i] != blk_e_ref[jnp.maximum(i - 1, 0)])

    @pl.when(jnp.logical_and(used, first))
    def _():
        wgu_b[...] = wgu_ref[0].astype(BF16)
        wdn_b[...] = wdn_ref[0].astype(BF16)

    @pl.when(used)
    def _():
        hgu = _dot(x_ref[...].astype(BF16), wgu_b[...]) + bgu_ref[0]
        glu = jnp.minimum(hgu[:, 0:D_FF], SWIGLU_LIMIT)
        lin = jnp.clip(hgu[:, D_FF:2 * D_FF], -SWIGLU_LIMIT, SWIGLU_LIMIT)
        act = glu * jax.nn.sigmoid(SWIGLU_ALPHA * glu) * (lin + 1.0)
        y_ref[...] = _dot(act.astype(BF16), wdn_b[...]) + bdn_ref[0]


def _experts(blk_e, n_used, xs, w_gu, b_gu, w_dn, b_dn):
    n_rows = xs.shape[0]
    blk = lambda i, be, nu: jnp.minimum(i, nu[0] - 1)
    exp = lambda i, be, nu: be[jnp.minimum(i, nu[0] - 1)]
    grid_spec = pltpu.PrefetchScalarGridSpec(
        num_scalar_prefetch=2, grid=(n_rows // MOE_TM,),
        in_specs=[pl.BlockSpec((MOE_TM, D_MODEL), lambda i, be, nu: (blk(i, be, nu), 0)),
                  pl.BlockSpec((1, D_MODEL, 2 * D_FF), lambda i, be, nu: (exp(i, be, nu), 0, 0)),
                  pl.BlockSpec((1, 1, 2 * D_FF), lambda i, be, nu: (exp(i, be, nu), 0, 0)),
                  pl.BlockSpec((1, D_FF, D_MODEL), lambda i, be, nu: (exp(i, be, nu), 0, 0)),
                  pl.BlockSpec((1, 1, D_MODEL), lambda i, be, nu: (exp(i, be, nu), 0, 0))],
        out_specs=pl.BlockSpec((MOE_TM, D_MODEL), lambda i, be, nu: (i, 0)),
        scratch_shapes=[pltpu.VMEM((D_MODEL, 2 * D_FF), BF16), pltpu.VMEM((D_FF, D_MODEL), BF16)])
    return pl.pallas_call(
        _expert_kernel, grid_spec=grid_spec,
        out_shape=jax.ShapeDtypeStruct((n_rows, D_MODEL), F32),
        compiler_params=_cparams("arbitrary"), name="moe_experts",
    )(blk_e, n_used, xs, w_gu, b_gu, w_dn, b_dn)


def _combine_kernel(dest_ref, next_ref, gate_ref, x_ref, g_ref, b_ref, ys_ref, o_ref, buf, sems, *, dn_alpha):
    i = pl.program_id(0)
    slot = i % 2

    def issue(d_ref, s):
        def start(t, c):
            for k in range(TOP_K):
                pltpu.make_async_copy(ys_ref.at[pl.ds(d_ref[0, 0, t * TOP_K + k], 1)],
                                      buf.at[s, k, pl.ds(t, 1)], sems.at[s]).start(priority=k % 2)
            return c

        lax.fori_loop(0, TOK_BLOCK, start, 0, unroll=8)

    @pl.when(i == 0)
    def _():
        issue(dest_ref, 0)

    @pl.when(i + 1 < pl.num_programs(0))
    def _():
        issue(next_ref, 1 - slot)

    for k in range(TOP_K):
        pltpu.make_async_copy(ys_ref.at[pl.ds(0, TOK_BLOCK)], buf.at[slot, k], sems.at[slot]).wait()
    gate = gate_ref[...]
    y = dn_alpha * x_ref[...]
    for k in range(TOP_K):
        y = y + gate[:, k:k + 1] * buf[slot, k]
    o_ref[...] = _layer_norm(y, g_ref[...], b_ref[...])


def _combine(dest3, gates, x1, g, b, ys, dn_alpha):
    t = x1.shape[0]
    n = t // TOK_BLOCK
    tok = pl.BlockSpec((TOK_BLOCK, D_MODEL), lambda i: (i, 0))
    vec = pl.BlockSpec((1, D_MODEL), lambda i: (0, 0))
    return pl.pallas_call(
        functools.partial(_combine_kernel, dn_alpha=dn_alpha), grid=(n,),
        in_specs=[pl.BlockSpec((1, 1, ROWS_PER_STEP), lambda i: (i, 0, 0), memory_space=pltpu.SMEM),
                  pl.BlockSpec((1, 1, ROWS_PER_STEP), lambda i: (jnp.minimum(i + 1, n - 1), 0, 0),
                               memory_space=pltpu.SMEM),
                  pl.BlockSpec((TOK_BLOCK, LANE), lambda i: (i, 0)), tok, vec, vec,
                  pl.BlockSpec(memory_space=pl.ANY)],
        out_specs=tok,
        out_shape=jax.ShapeDtypeStruct((t, D_MODEL), F32),
        scratch_shapes=[pltpu.VMEM((2, TOP_K, TOK_BLOCK, D_MODEL), F32), pltpu.SemaphoreType.DMA((2,))],
        compiler_params=_cparams("arbitrary"), name="moe_combine_ln",
    )(dest3, dest3, gates, x1, g, b, ys)


def _onehots(idx):
    lane = lax.broadcasted_iota(jnp.int32, idx.shape, 1)
    return lane, [lane == idx[:, k:k + 1] for k in range(TOP_K)]


def _lanes(cols, lane):
    out = jnp.zeros(lane.shape, jnp.int32)
    for k, c in enumerate(cols):
        out = jnp.where(lane == k, c, out)
    return out


def _rank_kernel(cnt0_ref, idx_ref, rank_ref, cnt_ref, carry_ref):
    @pl.when(pl.program_id(0) == 0)
    def _():
        carry_ref[...] = cnt0_ref[...]

    n = TOK_BLOCK
    r = lax.broadcasted_iota(jnp.int32, (n, n), 0)
    c = lax.broadcasted_iota(jnp.int32, (n, n), 1)
    earlier = jnp.where(c < r, 1.0, 0.0).astype(BF16)
    carry = carry_ref[...]
    for sub in range(idx_ref.shape[0] // n):
        rows = slice(sub * n, (sub + 1) * n)
        lane, hots = _onehots(idx_ref[rows, :])
        chose = sum(h.astype(F32) for h in hots)
        before = _dot(earlier, chose.astype(BF16)) + carry
        ranks = [jnp.sum(jnp.where(h, before, 0.0), axis=-1, keepdims=True).astype(jnp.int32) for h in hots]
        rank_ref[rows, :] = _lanes(ranks, lane)
        carry = carry + jnp.sum(chose, axis=0, keepdims=True)
    carry_ref[...] = carry
    cnt_ref[...] = carry


def _route_block(t):
    return 1024 if t % 1024 == 0 else TOK_BLOCK


def _rank(cnt0, idx):
    t = idx.shape[0]
    tb = _route_block(t)
    tok = pl.BlockSpec((tb, LANE), lambda i: (i, 0))
    one = pl.BlockSpec((1, LANE), lambda i: (0, 0))
    return pl.pallas_call(
        _rank_kernel, grid=(t // tb,), in_specs=[one, tok], out_specs=[tok, one],
        out_shape=[jax.ShapeDtypeStruct((t, LANE), jnp.int32), jax.ShapeDtypeStruct((1, LANE), F32)],
        scratch_shapes=[pltpu.VMEM((1, LANE), F32)],
        compiler_params=_cparams("arbitrary"), name="moe_rank",
    )(cnt0, idx)


def _dest_kernel(cnt_ref, idx_ref, rank_ref, dest_ref, blk_ref, used_ref, start_ref):
    @pl.when(pl.program_id(0) == 0)
    def _():
        lane = lax.broadcasted_iota(jnp.int32, (8, LANE), 1)
        cnt = jnp.broadcast_to(cnt_ref[...], (8, LANE)).astype(jnp.int32)
        nblk = jnp.where(lane < N_EXPERTS, jnp.right_shift(cnt + (MOE_TM - 1), MOE_TM.bit_length() - 1), 0)
        end = nblk
        for s in (1, 2, 4, 8, 16):
            end = end + jnp.where(lane >= s, pltpu.roll(end, s, 1), 0)
        start_ref[...] = ((end - nblk) * MOE_TM).astype(F32)
        used_ref[...] = jnp.broadcast_to(end[:, N_EXPERTS - 1:N_EXPERTS], (8, LANE))
        blk = lax.broadcasted_iota(jnp.int32, blk_ref.shape, 0)
        lane_b = lax.broadcasted_iota(jnp.int32, blk_ref.shape, 1)
        done = jnp.where(lane_b < N_EXPERTS, jnp.where(end[0:1, :] <= blk, 1.0, 0.0), 0.0)
        blk_ref[...] = jnp.broadcast_to(
            jnp.minimum(jnp.sum(done, axis=-1, keepdims=True), N_EXPERTS - 1.0).astype(jnp.int32), blk_ref.shape)

    lane, hots = _onehots(idx_ref[...])
    start = start_ref[0:1, :]
    rank = rank_ref[...]
    dest_ref[...] = _lanes([jnp.sum(jnp.where(h, start, 0.0), axis=-1, keepdims=True).astype(jnp.int32)
                            + rank[:, k:k + 1] for k, h in enumerate(hots)], lane)


def _dest(cnt, idx, rank, n_blocks):
    t = idx.shape[0]
    tb = _route_block(t)
    tok = pl.BlockSpec((tb, LANE), lambda i: (i, 0))
    rows = -(-n_blocks // 8) * 8
    return pl.pallas_call(
        _dest_kernel, grid=(t // tb,),
        in_specs=[pl.BlockSpec((1, LANE), lambda i: (0, 0)), tok, tok],
        out_specs=[tok, pl.BlockSpec((rows, LANE), lambda i: (0, 0)), pl.BlockSpec((8, LANE), lambda i: (0, 0))],
        out_shape=[jax.ShapeDtypeStruct((t, LANE), jnp.int32), jax.ShapeDtypeStruct((rows, LANE), jnp.int32),
                   jax.ShapeDtypeStruct((8, LANE), jnp.int32)],
        scratch_shapes=[pltpu.VMEM((8, LANE), F32)],
        compiler_params=_cparams("arbitrary"), name="moe_dest",
    )(cnt, idx, rank)


def _moe(groups, w_gu, b_gu, w_dn, b_dn, g, b, dn_alpha):
    n_assign = sum(x1.shape[0] for x1, _, _ in groups) * TOP_K
    n_blocks = n_assign // MOE_TM + N_EXPERTS
    cnt = jnp.zeros((1, LANE), F32)
    ranks = []
    for _, idx, _ in groups:
        rank, cnt = _rank(cnt, idx)
        ranks.append(rank)
    xs = jnp.zeros((n_blocks * MOE_TM, D_MODEL), F32)
    dests = []
    for (x1, idx, _), rank in zip(groups, ranks):
        dest, blk, used = _dest(cnt, idx, rank, n_blocks)
        dest3 = dest[:, :TOP_K].reshape(-1, 1, TOK_BLOCK * TOP_K)
        xs = _scatter_rows(dest3, x1, xs)
        dests.append(dest3)
    ys = _experts(blk[:n_blocks, 0], used[0, :1], xs, w_gu, b_gu, w_dn, b_dn)
    return [_combine(dest3, gates, x1, g, b, ys, dn_alpha) for (x1, _, gates), dest3 in zip(groups, dests)]


def _project_all(x, w_in, pos, tm_plain, tm_rope, keys_t=None, operand_dtype=F32):
    ret_tab = [jnp.concatenate(p, axis=1) for p in zip(_rope_table(pos, R_DK, R_HEADS),
                                                       _rope_table(pos, R_DK, R_HEADS, R_DK ** -0.5))]
    diff_tab = _rope_table(pos, D_DH, 2 * D_HEADS)
    qk = _proj(x, w_in, 0, COL, tm_rope, (ret_tab[0], ret_tab[1], R_DK))
    vr = _proj(x, w_in, 1 * COL, COL, tm_plain, out_dtype=operand_dtype)
    gr = _proj(x, w_in, 2 * COL, COL, tm_plain)
    qd = _proj(x, w_in, 3 * COL, COL, tm_rope, (diff_tab[0], diff_tab[1], D_DH), out_dtype=operand_dtype)
    if keys_t is None:
        kd = _proj(x, w_in, 4 * COL, COL, tm_rope, (diff_tab[0], diff_tab[1], D_DH))
    else:
        cos_t, sin_t = (t[:, :D_DH].T for t in diff_tab)
        kd = _proj_keys_t(x, w_in, 4 * COL, tm_rope, cos_t, sin_t, *keys_t)
    vd = _proj(x, w_in, 5 * COL, COL, tm_plain)
    ga = _proj(x, w_in, 6 * COL, COL, tm_plain)
    gb = _proj(x, w_in, 7 * COL, COL, tm_plain)
    return qk, vr, gr, qd, kd, vd, ga, gb


def kernel(x_prompt, x_sample, cache_k, cache_v, state_ret, page_table, w_in, w_branch_ret, w_branch_diff, w_out, lam_q1, lam_k1, lam_q2, lam_k2, subln_w, ln1_g, ln1_b, w_router, b_router, w_gate_up, b_gate_up, w_down, b_down, ln2_g, ln2_b):
    depth = w_in.shape[0]
    assert depth == 1, "single-layer trunk"
    batch, seq, _ = x_prompt.shape
    dbatch, dseq, _ = x_sample.shape
    dn_alpha = (2.0 * depth) ** 0.25
    lam_init = 0.8 - 0.6 * math.exp(-0.3 * 0)
    n_p, n_s = batch * seq, dbatch * dseq

    w_in_b = w_in[0].astype(BF16)
    wr_b, wd_b, wo_b = w_branch_ret[0].astype(BF16), w_branch_diff[0].astype(BF16), w_out[0].astype(BF16)
    w_rt_b = w_router[0].astype(BF16)
    lam_args = (lam_q1, lam_k1, lam_q2, lam_k2, subln_w)

    xp = x_prompt.reshape(n_p, D_MODEL)
    qk, vr, gr, qd, kd_t, vd, ga, gb = _project_all(xp.astype(BF16), w_in_b, jnp.arange(seq), 1024, 512,
                                                    (batch, seq), BF16)
    o_r, s_p = _retention_prompt(qk, vr, gr, batch, seq)
    o_d = _dattn_prompt(lam_args, qd, kd_t, vd, batch, seq, lam_init)
    x1_p, idx_p, gate_p = _mix(xp, o_r, o_d, ga, gb, wr_b, wd_b, wo_b, ln1_g, ln1_b, w_rt_b, b_router, 256, dn_alpha)

    xs_ = x_sample.reshape(n_s, D_MODEL)
    pos_s = jnp.tile(PAST_LEN + jnp.arange(dseq), dbatch)
    qk_s, vr_s, gr_s, qd_s, kd_s, vd_s, ga_s, gb_s = _project_all(xs_, w_in_b, pos_s, n_s, n_s)
    pad8 = lambda a: jnp.pad(a.reshape(dbatch, dseq, -1), ((0, 0), (0, 8 - dseq), (0, 0)))
    pad_page = lambda a: jnp.pad(a.reshape(dbatch, dseq, -1), ((0, 0), (0, PAGE_SIZE - dseq), (0, 0)))
    o_r_s, s_s = _retention_sample(pad8(qk_s), pad8(vr_s), pad8(gr_s), state_ret[0], dseq)
    n_pool = cache_k.shape[1]
    cache_kt = jnp.transpose(cache_k[0], (0, 2, 3, 4, 1)).reshape(n_pool, D_HEADS * 2 * D_DH, PAGE_SIZE)
    cache_vr = cache_v[0].reshape(n_pool, PAGE_SIZE * D_HEADS, D_DV)
    kn_t = jnp.transpose(pad_page(kd_s), (0, 2, 1))
    vn = pad_page(vd_s).reshape(dbatch, PAGE_SIZE * D_HEADS, D_DV)
    o_d_s = _dattn_decode(lam_args, page_table, pad8(qd_s), kn_t, vn, cache_kt, cache_vr, lam_init)
    o_r_s = o_r_s[:, :dseq].reshape(n_s, -1)
    o_d_s = o_d_s[:, :dseq].reshape(n_s, -1)
    x1_s, idx_s, gate_s = _mix(xs_, o_r_s, o_d_s, ga_s, gb_s, wr_b, wd_b, wo_b, ln1_g, ln1_b, w_rt_b, b_router,
                               n_s, dn_alpha)

    y_p, y_s = _moe([(x1_p, idx_p, gate_p), (x1_s, idx_s, gate_s)],
                    w_gate_up[0], b_gate_up[0][:, None, :], w_down[0], b_down[0][:, None, :], ln2_g, ln2_b, dn_alpha)

    return (y_p.reshape(batch, seq, D_MODEL),
            y_s.reshape(dbatch, dseq, D_MODEL),
            jnp.transpose(kd_t.reshape(batch, D_HEADS, 2, D_DH, seq), (0, 4, 1, 2, 3))[None],
            vd.reshape(1, batch, seq, D_HEADS, D_DV),
            s_p[None],
            kd_s.reshape(1, dbatch, dseq, D_HEADS, 2, D_DH),
            vd_s.reshape(1, dbatch, dseq, D_HEADS, D_DV),
            s_s[None])
```

```python
import functools
import math

import jax
import jax.numpy as jnp
from jax import lax
from jax.experimental import pallas as pl
from jax.experimental.pallas import tpu as pltpu

F32 = jnp.float32
BF16 = jnp.bfloat16

D_MODEL = 1024
PAST_LEN = 16384
PAGE_SIZE = 128
R_HEADS, R_DK, R_DV, R_CHUNK = 4, 128, 256, 128
D_HEADS, D_DH, D_DV = 8, 64, 128
ROPE_THETA = 10000.0
N_EXPERTS, TOP_K, D_FF = 32, 4, 1024
SWIGLU_ALPHA, SWIGLU_LIMIT = 1.702, 7.0
NEG_INF = -1e30

LANE = 128
VMEM_LIMIT = 56 * 1024 * 1024
ATTN_BLOCK = 512
DEC_PAGES = 16
MOE_TM = 512
TOK_BLOCK = 128
COL = 1024


def _cparams(*sem):
    return pltpu.CompilerParams(dimension_semantics=sem, vmem_limit_bytes=VMEM_LIMIT)


def _dot(a, b):
    return jnp.dot(a, b, preferred_element_type=F32)


def _dot_nt(a, b):
    return lax.dot_general(a, b, (((1,), (1,)), ((), ())), preferred_element_type=F32)


def _swap_halves(sl, unit):
    if unit == LANE:
        return pltpu.roll(sl, LANE // 2, 1)
    lane = lax.broadcasted_iota(jnp.int32, sl.shape, 1)
    fwd = pltpu.roll(sl, unit // 2, 1)
    bwd = pltpu.roll(sl, LANE - unit // 2, 1)
    return jnp.where((lane % unit) < unit // 2, bwd, fwd)


def _proj_kernel(x_ref, w_ref, o_ref):
    o_ref[...] = _dot(x_ref[...].astype(BF16), w_ref[...]).astype(o_ref.dtype)


def _proj_rope_kernel(x_ref, w_ref, cos_ref, sin_ref, o_ref, *, unit, scales):
    h = _dot(x_ref[...].astype(BF16), w_ref[...])
    cos, sin = cos_ref[...], sin_ref[...]
    for g, scale in enumerate(scales):
        sl = h[:, g * LANE:(g + 1) * LANE]
        o = sl * cos + _swap_halves(sl, unit) * sin
        if scale != 1.0:
            o = o * scale
        o_ref[:, g * LANE:(g + 1) * LANE] = o.astype(o_ref.dtype)


def _proj_rope_t_kernel(x_ref, w_ref, cos_ref, sin_ref, o_ref):
    ht = _dot(x_ref[...].astype(BF16), w_ref[...]).T
    half = D_DH // 2
    cos, sin = cos_ref[...], sin_ref[...]
    parts = []
    for u in range(ht.shape[0] // D_DH):
        blk = ht[u * D_DH:(u + 1) * D_DH]
        swapped = jnp.concatenate([blk[half:], blk[:half]], axis=0)
        parts.append(blk * cos + swapped * sin)
    o_ref[...] = jnp.concatenate(parts, axis=0)


def _proj_keys_t(x, w, col0, tm, cos_t, sin_t, batch, seq):
    n_pos = seq // tm
    return pl.pallas_call(
        _proj_rope_t_kernel, grid=(batch * n_pos,),
        in_specs=[pl.BlockSpec((tm, D_MODEL), lambda i: (i, 0)),
                  pl.BlockSpec((D_MODEL, COL), lambda i: (0, col0 // COL)),
                  pl.BlockSpec((D_DH, tm), lambda i: (0, i % n_pos)),
                  pl.BlockSpec((D_DH, tm), lambda i: (0, i % n_pos))],
        out_specs=pl.BlockSpec((COL, tm), lambda i: (i // n_pos, i % n_pos)),
        out_shape=jax.ShapeDtypeStruct((batch * COL, seq), F32),
        compiler_params=_cparams("parallel"), name="proj_keys_t",
    )(x, w, cos_t, sin_t)


def _proj(x, w, col0, ncols, tm, rope=None, out_dtype=F32):
    t = x.shape[0]
    grid = (t // tm, ncols // COL)
    in_specs = [pl.BlockSpec((tm, D_MODEL), lambda i, j: (i, 0)),
                pl.BlockSpec((D_MODEL, COL), lambda i, j: (0, col0 // COL + j))]
    args = [x, w]
    if rope is None:
        body = _proj_kernel
    else:
        cos, sin, unit, scales = rope
        n_pos = cos.shape[0] // tm
        in_specs += [pl.BlockSpec((tm, LANE), lambda i, j: (i % n_pos, 0))] * 2
        args += [cos, sin]
        body = functools.partial(_proj_rope_kernel, unit=unit, scales=scales)
    return pl.pallas_call(
        body, grid=grid, in_specs=in_specs,
        out_specs=pl.BlockSpec((tm, COL), lambda i, j: (i, j)),
        out_shape=jax.ShapeDtypeStruct((t, ncols), out_dtype),
        compiler_params=_cparams("parallel", "arbitrary"), name="proj" if rope is None else "proj_rope",
    )(*args)


def _rope_table(pos, unit):
    half = unit // 2
    inv = ROPE_THETA ** (-jnp.arange(half, dtype=F32) / half)
    ang = pos.astype(F32)[:, None] * inv[None, :]
    cos, sin = jnp.cos(ang), jnp.sin(ang)
    c = jnp.concatenate([cos, cos], axis=-1)
    s = jnp.concatenate([-sin, sin], axis=-1)
    return jnp.tile(c, (1, LANE // unit)), jnp.tile(s, (1, LANE // unit))


def _ret_tables(chunk, n_tok):
    log_gamma = jnp.log(1.0 - 2.0 ** (-5.0 - jnp.arange(R_HEADS, dtype=F32)))
    idx = jnp.arange(chunk, dtype=F32)
    diff = idx[:, None] - idx[None, :]
    causal = diff >= 0
    dec = jnp.where(causal[None], jnp.exp(log_gamma[:, None, None] * jnp.where(causal, diff, 0.0)[None]), 0.0)
    qdec = jnp.exp(log_gamma[:, None] * (idx[None, :] + 1.0))
    kdec = jnp.exp(log_gamma[:, None] * (n_tok - 1.0 - idx[None, :]))
    sdec = jnp.exp(log_gamma * n_tok)
    return (dec,
            jnp.broadcast_to(qdec[:, :, None], (R_HEADS, chunk, R_DV)),
            jnp.broadcast_to(kdec[:, :, None], (R_HEADS, chunk, R_DK)),
            jnp.broadcast_to(sdec[:, None, None], (R_HEADS, 1, R_DV)))


def _ret_chunk(q, k, v, g, state, dec, qdec, kdec, sdec):
    qb, kb, vb = q.astype(BF16), k.astype(BF16), v.astype(BF16)
    scores = _dot_nt(qb, kb) * dec
    inner = _dot(scores.astype(BF16), vb)
    cross = _dot(qb, state.astype(BF16)) * qdec
    o = inner + cross
    kd_t = (k * kdec).T.astype(BF16)
    s_new = state * sdec + _dot(kd_t, vb)
    mu = jnp.mean(o, axis=-1, keepdims=True)
    var = jnp.mean(jnp.square(o - mu), axis=-1, keepdims=True)
    o = (o - mu) * lax.rsqrt(var + 1e-6)
    return o * (g * jax.nn.sigmoid(g)), s_new


def _ret_prompt_kernel(q_ref, k_ref, v_ref, g_ref, dec_ref, qdec_ref, kdec_ref, sdec_ref, o_ref, s_ref):
    @pl.when(pl.program_id(1) == 0)
    def _():
        s_ref[...] = jnp.zeros_like(s_ref)

    for b in range(RET_SEQS):
        for h in range(R_HEADS):
            dk, dv = slice(h * R_DK, (h + 1) * R_DK), slice(h * R_DV, (h + 1) * R_DV)
            o, s_new = _ret_chunk(q_ref[b, :, dk], k_ref[b, :, dk], v_ref[b, :, dv], g_ref[b, :, dv], s_ref[b, h],
                                  dec_ref[h], qdec_ref[h], kdec_ref[h], sdec_ref[h])
            o_ref[b, :, dv] = o.astype(o_ref.dtype)
            s_ref[b, h] = s_new


RET_SEQS = 2


def _retention_prompt(qk, vr, gr, batch, seq):
    nc = seq // R_CHUNK
    tabs = _ret_tables(R_CHUNK, R_CHUNK)
    whole = lambda a: pl.BlockSpec(a.shape, lambda b, c: (0,) * a.ndim)
    seq3 = lambda a: a.reshape(batch, seq, a.shape[-1])
    o, s = pl.pallas_call(
        _ret_prompt_kernel, grid=(batch // RET_SEQS, nc),
        in_specs=[
            pl.BlockSpec((RET_SEQS, R_CHUNK, R_HEADS * R_DK), lambda b, c: (b, c, 0)),
            pl.BlockSpec((RET_SEQS, R_CHUNK, R_HEADS * R_DK), lambda b, c: (b, c, 1)),
            pl.BlockSpec((RET_SEQS, R_CHUNK, R_HEADS * R_DV), lambda b, c: (b, c, 0)),
            pl.BlockSpec((RET_SEQS, R_CHUNK, R_HEADS * R_DV), lambda b, c: (b, c, 0)),
        ] + [whole(a) for a in tabs],
        out_specs=[
            pl.BlockSpec((RET_SEQS, R_CHUNK, R_HEADS * R_DV), lambda b, c: (b, c, 0)),
            pl.BlockSpec((RET_SEQS, R_HEADS, R_DK, R_DV), lambda b, c: (b, 0, 0, 0)),
        ],
        out_shape=[jax.ShapeDtypeStruct((batch, seq, R_HEADS * R_DV), BF16),
                   jax.ShapeDtypeStruct((batch, R_HEADS, R_DK, R_DV), F32)],
        compiler_params=_cparams("parallel", "arbitrary"), name="retention_prompt",
    )(seq3(qk), seq3(qk), seq3(vr), seq3(gr), *tabs)
    return o.reshape(batch * seq, -1), s


def _ret_sample_kernel(q_ref, k_ref, v_ref, g_ref, s0_ref, dec_ref, qdec_ref, kdec_ref, sdec_ref,
                       o_ref, s_ref, qp, kp, vp, gp):
    n = q_ref.shape[1]
    for pad, src in ((qp, q_ref), (kp, k_ref), (vp, v_ref), (gp, g_ref)):
        pad[...] = jnp.zeros_like(pad)
        pad[0:n, :] = src[0]
    o, s_new = _ret_chunk(qp[...], kp[...], vp[...], gp[...], s0_ref[0, 0],
                          dec_ref[0], qdec_ref[0], kdec_ref[0], sdec_ref[0])
    o_ref[0] = o[0:n, :]
    s_ref[0, 0] = s_new


def _retention_sample(qk, vr, gr, state, n_tok):
    batch, rows, _ = qk.shape
    tabs = _ret_tables(R_CHUNK, n_tok)
    return pl.pallas_call(
        _ret_sample_kernel, grid=(batch, R_HEADS),
        in_specs=[
            pl.BlockSpec((1, rows, R_DK), lambda b, h: (b, 0, h)),
            pl.BlockSpec((1, rows, R_DK), lambda b, h: (b, 0, R_HEADS + h)),
            pl.BlockSpec((1, rows, R_DV), lambda b, h: (b, 0, h)),
            pl.BlockSpec((1, rows, R_DV), lambda b, h: (b, 0, h)),
            pl.BlockSpec((1, 1, R_DK, R_DV), lambda b, h: (b, h, 0, 0)),
            pl.BlockSpec((1, R_CHUNK, R_CHUNK), lambda b, h: (h, 0, 0)),
            pl.BlockSpec((1, R_CHUNK, R_DV), lambda b, h: (h, 0, 0)),
            pl.BlockSpec((1, R_CHUNK, R_DK), lambda b, h: (h, 0, 0)),
            pl.BlockSpec((1, 1, R_DV), lambda b, h: (h, 0, 0)),
        ],
        out_specs=[
            pl.BlockSpec((1, rows, R_DV), lambda b, h: (b, 0, h)),
            pl.BlockSpec((1, 1, R_DK, R_DV), lambda b, h: (b, h, 0, 0)),
        ],
        out_shape=[jax.ShapeDtypeStruct((batch, rows, R_HEADS * R_DV), F32),
                   jax.ShapeDtypeStruct((batch, R_HEADS, R_DK, R_DV), F32)],
        scratch_shapes=[pltpu.VMEM((R_CHUNK, R_DK), F32), pltpu.VMEM((R_CHUNK, R_DK), F32),
                        pltpu.VMEM((R_CHUNK, R_DV), F32), pltpu.VMEM((R_CHUNK, R_DV), F32)],
        compiler_params=_cparams("parallel", "parallel"), name="retention_sample",
    )(qk, qk, vr, gr, state, *tabs)


def _lambda(lq1_ref, lk1_ref, lq2_ref, lk2_ref, lam_init):
    a = jnp.sum(lq1_ref[...] * lk1_ref[...], axis=-1, keepdims=True)
    b = jnp.sum(lq2_ref[...] * lk2_ref[...], axis=-1, keepdims=True)
    return jnp.exp(a) - jnp.exp(b) + lam_init


def _sub_rms(o, w, lam_init):
    ms = jnp.mean(jnp.square(o), axis=-1, keepdims=True)
    return o * lax.rsqrt(ms + 1e-5) * w * (1.0 - lam_init)


def _dattn_prompt_kernel(lq1_ref, lk1_ref, lq2_ref, lk2_ref, w_ref, q_ref, kt_ref, v_ref, o_ref,
                         kc_ref, vx_ref, *, blk, lam_init):
    kc_ref[...] = kt_ref[...].astype(BF16)
    vx_ref[:, 0:D_DV] = v_ref[...].astype(BF16)
    vx_ref[:, D_DV:2 * D_DV] = jnp.ones((vx_ref.shape[0], D_DV), BF16)
    row = lax.broadcasted_iota(jnp.int32, (blk, blk), 0)
    col = lax.broadcasted_iota(jnp.int32, (blk, blk), 1)
    lam = _lambda(lq1_ref, lk1_ref, lq2_ref, lk2_ref, lam_init)

    for qi in range(q_ref.shape[0] // blk):
        q = q_ref[qi * blk:(qi + 1) * blk, :] * (D_DH ** -0.5)
        outs = []
        for c in range(2):
            qc = q[:, c * D_DH:(c + 1) * D_DH].astype(BF16)
            m = jnp.full((blk, 1), NEG_INF, F32)
            acc = jnp.zeros((blk, 2 * D_DV), F32)
            for j in range(qi + 1):
                s = _dot(qc, kc_ref[c * D_DH:(c + 1) * D_DH, j * blk:(j + 1) * blk])
                if j == qi:
                    s = jnp.where(col <= row, s, NEG_INF)
                m_new = jnp.maximum(m, jnp.max(s, axis=-1, keepdims=True))
                p = jnp.exp(s - m_new)
                acc = acc * jnp.exp(m - m_new) + _dot(p.astype(BF16), vx_ref[j * blk:(j + 1) * blk, :])
                m = m_new
            outs.append(acc[:, 0:D_DV] / acc[:, D_DV:D_DV + 1])
        o_ref[qi * blk:(qi + 1) * blk, :] = _sub_rms(outs[0] - lam * outs[1], w_ref[...],
                                                     lam_init).astype(o_ref.dtype)


def _lam_specs(n):
    zero = lambda *_: (0, 0)
    return [pl.BlockSpec((1, D_DH), zero)] * 4 + [pl.BlockSpec((1, D_DV), zero)]


def _dattn_prompt(lam_args, qd, kd_t, vd, batch, seq, lam_init):
    blk = min(ATTN_BLOCK, seq)
    return pl.pallas_call(
        functools.partial(_dattn_prompt_kernel, blk=blk, lam_init=lam_init),
        grid=(batch, D_HEADS),
        in_specs=_lam_specs(2) + [
            pl.BlockSpec((seq, 2 * D_DH), lambda b, h: (b, h)),
            pl.BlockSpec((2 * D_DH, seq), lambda b, h: (b * D_HEADS + h, 0)),
            pl.BlockSpec((seq, D_DV), lambda b, h: (b, h)),
        ],
        out_specs=pl.BlockSpec((seq, D_DV), lambda b, h: (b, h)),
        out_shape=jax.ShapeDtypeStruct((batch * seq, D_HEADS * D_DV), BF16),
        scratch_shapes=[pltpu.VMEM((2 * D_DH, seq), BF16), pltpu.VMEM((seq, 2 * D_DV), BF16)],
        compiler_params=_cparams("parallel", "parallel"), name="dattn_prompt",
    )(*lam_args, qd, kd_t, vd)


DEC_ROWS = D_HEADS * 2 * 8


def _dattn_decode_kernel(pt_ref, lq1_ref, lk1_ref, lq2_ref, lk2_ref, w_ref, q_ref, kn_ref, vn_ref, ck_ref, cv_ref,
                         o_ref, qbd_ref, m_ref, l_ref, acc_ref, kbuf, vbuf, sems, *, n_pages, lam_init):
    j = pl.program_id(1)
    step = pl.program_id(0) * pl.num_programs(1) + j
    slot = step % 2

    def fetch(s, into):
        for p in range(n_pages):
            page = pt_ref[s * n_pages + p]
            pltpu.make_async_copy(ck_ref.at[page], kbuf.at[into, p], sems.at[0, into]).start()
            pltpu.make_async_copy(cv_ref.at[page], vbuf.at[into, p], sems.at[1, into]).start()

    @pl.when(step == 0)
    def _():
        fetch(0, 0)

    @pl.when(step + 1 < pl.num_programs(0) * pl.num_programs(1))
    def _():
        fetch(step + 1, 1 - slot)

    pltpu.make_async_copy(ck_ref.at[pl.ds(0, n_pages)], kbuf.at[slot], sems.at[0, slot]).wait()
    pltpu.make_async_copy(cv_ref.at[pl.ds(0, n_pages)], vbuf.at[slot], sems.at[1, slot]).wait()

    @pl.when(j == 0)
    def _():
        q8 = q_ref[0] * (D_DH ** -0.5)
        tiled = jnp.concatenate([q8] * (DEC_ROWS // 8), axis=0)
        r = lax.broadcasted_iota(jnp.int32, tiled.shape, 0)
        cidx = lax.broadcasted_iota(jnp.int32, tiled.shape, 1)
        qbd_ref[...] = jnp.where(r // 8 == cidx // D_DH, tiled, 0.0).astype(BF16)
        m_ref[...] = jnp.full(m_ref.shape, NEG_INF, F32)
        l_ref[...] = jnp.zeros(l_ref.shape, F32)
        acc_ref[...] = jnp.zeros(acc_ref.shape, F32)

    rows_h = DEC_ROWS // D_HEADS

    def update(s_list, v_pages):
        m = m_ref[...]
        m_new = m
        for s in s_list:
            m_new = jnp.maximum(m_new, jnp.max(s, axis=-1, keepdims=True))
        alpha = jnp.exp(m - m_new)
        p = jnp.concatenate([jnp.exp(s - m_new) for s in s_list], axis=1)
        l_ref[...] = l_ref[...] * alpha + jnp.sum(p, axis=-1, keepdims=True)
        m_ref[...] = m_new
        p = p.astype(BF16)
        for h in range(D_HEADS):
            rows = slice(h * rows_h, (h + 1) * rows_h)
            v_h = jnp.concatenate([v[pl.ds(h, PAGE_SIZE, stride=D_HEADS), :] for v in v_pages], axis=0)
            acc_ref[rows, :] = acc_ref[rows, :] * alpha[rows, :] + _dot(p[rows, :], v_h.astype(BF16))

    qbd = qbd_ref[...]
    update([_dot(qbd, kbuf[slot, p].astype(BF16)) for p in range(n_pages)],
           [vbuf.at[slot, p] for p in range(n_pages)])

    @pl.when(j == pl.num_programs(1) - 1)
    def _():
        s = _dot(qbd, kn_ref[0].astype(BF16))
        t_q = lax.broadcasted_iota(jnp.int32, s.shape, 0) % 8
        t_k = lax.broadcasted_iota(jnp.int32, s.shape, 1)
        update([jnp.where(t_k <= t_q, s, NEG_INF)], [vn_ref.at[0]])
        acc = acc_ref[...] / l_ref[...]
        lam = _lambda(lq1_ref, lk1_ref, lq2_ref, lk2_ref, lam_init)
        w = w_ref[...]
        heads = []
        for h in range(D_HEADS):
            o = acc[h * rows_h:h * rows_h + 8, :] - lam * acc[h * rows_h + 8:(h + 1) * rows_h, :]
            heads.append(_sub_rms(o, w, lam_init))
        o_ref[0] = jnp.concatenate(heads, axis=1)


def _dattn_decode(lam_args, page_table, q8, kn_t, vn, cache_kt, cache_v, lam_init):
    batch, n_pages = page_table.shape
    steps = n_pages // DEC_PAGES
    width = q8.shape[-1]
    page = (1,) + cache_kt.shape[1:]
    pages = (2, DEC_PAGES) + cache_kt.shape[1:]
    zero = lambda b, j, pt: (0, 0)
    seq_spec = lambda shape: pl.BlockSpec(shape, lambda b, j, pt: (b, 0, 0))
    grid_spec = pltpu.PrefetchScalarGridSpec(
        num_scalar_prefetch=1, grid=(batch, steps),
        in_specs=([pl.BlockSpec((1, D_DH), zero)] * 4 + [pl.BlockSpec((1, D_DV), zero)]
                  + [seq_spec((1, 8, width)), seq_spec(page), seq_spec(page)]
                  + [pl.BlockSpec(memory_space=pl.ANY)] * 2),
        out_specs=seq_spec((1, 8, width)),
        scratch_shapes=[pltpu.VMEM((DEC_ROWS, width), BF16), pltpu.VMEM((DEC_ROWS, 1), F32),
                        pltpu.VMEM((DEC_ROWS, 1), F32), pltpu.VMEM((DEC_ROWS, D_DV), F32),
                        pltpu.VMEM(pages, F32), pltpu.VMEM(pages, F32), pltpu.SemaphoreType.DMA((2, 2))])
    return pl.pallas_call(
        functools.partial(_dattn_decode_kernel, n_pages=DEC_PAGES, lam_init=lam_init),
        grid_spec=grid_spec,
        out_shape=jax.ShapeDtypeStruct((batch, 8, width), F32),
        compiler_params=_cparams("arbitrary", "arbitrary"), name="dattn_decode",
    )(page_table.reshape(-1), *lam_args, q8, kn_t, vn, cache_kt, cache_v)


def _layer_norm(y, g, b):
    mu = jnp.mean(y, axis=-1, keepdims=True)
    var = jnp.mean(jnp.square(y - mu), axis=-1, keepdims=True)
    return (y - mu) * lax.rsqrt(var + 1e-5) * g + b


def _mix_kernel(x_ref, or_ref, od_ref, ga_ref, gb_ref, wr_ref, wd_ref, wo_ref, g_ref, b_ref, wrt_ref, brt_ref,
                x1_ref, idx_ref, gate_ref, *, dn_alpha):
    r = _dot(or_ref[...].astype(BF16), wr_ref[...])
    d = _dot(od_ref[...].astype(BF16), wd_ref[...])
    mix = jax.nn.sigmoid(ga_ref[...]) * r + jax.nn.sigmoid(gb_ref[...]) * d
    y = dn_alpha * x_ref[...] + _dot(mix.astype(BF16), wo_ref[...])
    x1 = _layer_norm(y, g_ref[...], b_ref[...])
    x1_ref[...] = x1
    vals = _dot(x1.astype(BF16), wrt_ref[...]) + brt_ref[...]
    col = lax.broadcasted_iota(jnp.int32, vals.shape, 1)
    lane = lax.broadcasted_iota(jnp.int32, idx_ref.shape, 1)
    idx_out = jnp.zeros(idx_ref.shape, jnp.int32)
    val_out = jnp.full(gate_ref.shape, NEG_INF, F32)
    for k in range(TOP_K):
        mx = jnp.max(vals, axis=-1, keepdims=True)
        first = jnp.min(jnp.where(vals == mx, col, N_EXPERTS), axis=-1, keepdims=True)
        idx_out = jnp.where(lane == k, first, idx_out)
        val_out = jnp.where(lane == k, mx, val_out)
        vals = jnp.where(col == first, NEG_INF, vals)
    e = jnp.exp(val_out - jnp.max(val_out, axis=-1, keepdims=True))
    idx_ref[...] = idx_out
    gate_ref[...] = e / jnp.sum(e, axis=-1, keepdims=True)


def _mix(x, o_r, o_d, ga, gb, wr, wd, wo, g, b, w_rt, b_rt, tm, dn_alpha):
    t = x.shape[0]
    tok = pl.BlockSpec((tm, D_MODEL), lambda i: (i, 0))
    mat = pl.BlockSpec((D_MODEL, D_MODEL), lambda i: (0, 0))
    vec = pl.BlockSpec((1, D_MODEL), lambda i: (0, 0))
    return pl.pallas_call(
        functools.partial(_mix_kernel, dn_alpha=dn_alpha), grid=(t // tm,),
        in_specs=[tok] * 5 + [mat] * 3 + [vec, vec,
                                          pl.BlockSpec((D_MODEL, N_EXPERTS), lambda i: (0, 0)),
                                          pl.BlockSpec((1, N_EXPERTS), lambda i: (0, 0))],
        out_specs=[tok, pl.BlockSpec((tm, LANE), lambda i: (i, 0)), pl.BlockSpec((tm, LANE), lambda i: (i, 0))],
        out_shape=[jax.ShapeDtypeStruct((t, D_MODEL), F32), jax.ShapeDtypeStruct((t, LANE), jnp.int32),
                   jax.ShapeDtypeStruct((t, LANE), F32)],
        compiler_params=_cparams("parallel"), name="mix_ln_router",
    )(x, o_r, o_d, ga, gb, wr, wd, wo, g, b, w_rt, b_rt)


ROWS_PER_STEP = TOK_BLOCK * TOP_K


SCATTER_SLOTS = 3


def _scatter_rows_kernel(dest_ref, x_ref, xs_in_ref, xs_ref, xbuf, load_sems, row_sems):
    del xs_in_ref
    i = pl.program_id(0)
    n = pl.num_programs(0)
    slot = i % SCATTER_SLOTS

    def load(step):
        s = step % SCATTER_SLOTS
        rows = pl.ds(pl.multiple_of(step * TOK_BLOCK, TOK_BLOCK), TOK_BLOCK)
        return pltpu.make_async_copy(x_ref.at[rows], xbuf.at[s], load_sems.at[s])

    @pl.when(i == 0)
    def _():
        load(0).start()

        @pl.when(n > 1)
        def _():
            load(1).start()

    load(i).wait()

    def start(t, c):
        for k in range(TOP_K):
            pltpu.make_async_copy(xbuf.at[slot, pl.ds(t, 1)], xs_ref.at[pl.ds(dest_ref[0, 0, t * TOP_K + k], 1)],
                                  row_sems.at[slot]).start(priority=k % 2)
        return c

    lax.fori_loop(0, TOK_BLOCK, start, 0, unroll=8)

    def drain(s):
        pltpu.make_async_copy(xs_ref.at[pl.ds(0, ROWS_PER_STEP)], xs_ref.at[pl.ds(ROWS_PER_STEP, ROWS_PER_STEP)],
                              row_sems.at[s]).wait()

    @pl.when(i > 0)
    def _():
        drain((i - 1) % SCATTER_SLOTS)

    @pl.when(i + 2 < n)
    def _():
        load(i + 2).start()

    @pl.when(i == n - 1)
    def _():
        drain(slot)


def _scatter_rows(dest3, x, xs):
    t = x.shape[0]
    return pl.pallas_call(
        _scatter_rows_kernel, grid=(t // TOK_BLOCK,),
        in_specs=[pl.BlockSpec((1, 1, ROWS_PER_STEP), lambda i: (i, 0, 0), memory_space=pltpu.SMEM),
                  pl.BlockSpec(memory_space=pl.ANY),
                  pl.BlockSpec(memory_space=pl.ANY)],
        out_specs=pl.BlockSpec(memory_space=pl.ANY),
        out_shape=jax.ShapeDtypeStruct(xs.shape, xs.dtype),
        scratch_shapes=[pltpu.VMEM((SCATTER_SLOTS, TOK_BLOCK, D_MODEL), F32),
                        pltpu.SemaphoreType.DMA((SCATTER_SLOTS,)), pltpu.SemaphoreType.DMA((SCATTER_SLOTS,))],
        input_output_aliases={2: 0},
        compiler_params=_cparams("arbitrary"), name="moe_scatter_rows",
    )(dest3, x, xs)


def _expert_kernel(blk_e_ref, n_used_ref, x_ref, wgu_ref, bgu_ref, wdn_ref, bdn_ref, y_ref, wgu_b, wdn_b):
    i = pl.program_id(0)
    used = i < n_used_ref[0]

    @pl.when(jnp.logical_not(used))
    def _():
        y_ref[...] = jnp.zeros_like(y_ref)

    first = jnp.logical_or(i == 0, blk_e_ref[i] != blk_e_ref[jnp.maximum(i - 1, 0)])

    @pl.when(jnp.logical_and(used, first))
    def _():
        wgu_b[...] = wgu_ref[0].astype(BF16)
        wdn_b[...] = wdn_ref[0].astype(BF16)

    @pl.when(used)
    def _():
        hgu = _dot(x_ref[...].astype(BF16), wgu_b[...]) + bgu_ref[0]
        glu = jnp.minimum(hgu[:, 0:D_FF], SWIGLU_LIMIT)
        lin = jnp.clip(hgu[:, D_FF:2 * D_FF], -SWIGLU_LIMIT, SWIGLU_LIMIT)
        act = glu * jax.nn.sigmoid(SWIGLU_ALPHA * glu) * (lin + 1.0)
        y_ref[...] = _dot(act.astype(BF16), wdn_b[...]) + bdn_ref[0]


def _experts(blk_e, n_used, xs, w_gu, b_gu, w_dn, b_dn):
    n_rows = xs.shape[0]
    blk = lambda i, be, nu: jnp.minimum(i, nu[0] - 1)
    exp = lambda i, be, nu: be[jnp.minimum(i, nu[0] - 1)]
    grid_spec = pltpu.PrefetchScalarGridSpec(
        num_scalar_prefetch=2, grid=(n_rows // MOE_TM,),
        in_specs=[pl.BlockSpec((MOE_TM, D_MODEL), lambda i, be, nu: (blk(i, be, nu), 0)),
                  pl.BlockSpec((1, D_MODEL, 2 * D_FF), lambda i, be, nu: (exp(i, be, nu), 0, 0)),
                  pl.BlockSpec((1, 1, 2 * D_FF), lambda i, be, nu: (exp(i, be, nu), 0, 0)),
                  pl.BlockSpec((1, D_FF, D_MODEL), lambda i, be, nu: (exp(i, be, nu), 0, 0)),
                  pl.BlockSpec((1, 1, D_MODEL), lambda i, be, nu: (exp(i, be, nu), 0, 0))],
        out_specs=pl.BlockSpec((MOE_TM, D_MODEL), lambda i, be, nu: (i, 0)),
        scratch_shapes=[pltpu.VMEM((D_MODEL, 2 * D_FF), BF16), pltpu.VMEM((D_FF, D_MODEL), BF16)])
    return pl.pallas_call(
        _expert_kernel, grid_spec=grid_spec,
        out_shape=jax.ShapeDtypeStruct((n_rows, D_MODEL), F32),
        compiler_params=_cparams("arbitrary"), name="moe_experts",
    )(blk_e, n_used, xs, w_gu, b_gu, w_dn, b_dn)


def _combine_kernel(dest_ref, next_ref, gate_ref, x_ref, g_ref, b_ref, ys_ref, o_ref, buf, sems, *, dn_alpha):
    i = pl.program_id(0)
    slot = i % 2

    def issue(d_ref, s):
        def start(t, c):
            for k in range(TOP_K):
                pltpu.make_async_copy(ys_ref.at[pl.ds(d_ref[0, 0, t * TOP_K + k], 1)],
                                      buf.at[s, k, pl.ds(t, 1)], sems.at[s]).start(priority=k % 2)
            return c

        lax.fori_loop(0, TOK_BLOCK, start, 0, unroll=8)

    @pl.when(i == 0)
    def _():
        issue(dest_ref, 0)

    @pl.when(i + 1 < pl.num_programs(0))
    def _():
        issue(next_ref, 1 - slot)

    for k in range(TOP_K):
        pltpu.make_async_copy(ys_ref.at[pl.ds(0, TOK_BLOCK)], buf.at[slot, k], sems.at[slot]).wait()
    gate = gate_ref[...]
    y = dn_alpha * x_ref[...]
    for k in range(TOP_K):
        y = y + gate[:, k:k + 1] * buf[slot, k]
    o_ref[...] = _layer_norm(y, g_ref[...], b_ref[...])


def _combine(dest3, gates, x1, g, b, ys, dn_alpha):
    t = x1.shape[0]
    n = t // TOK_BLOCK
    tok = pl.BlockSpec((TOK_BLOCK, D_MODEL), lambda i: (i, 0))
    vec = pl.BlockSpec((1, D_MODEL), lambda i: (0, 0))
    return pl.pallas_call(
        functools.partial(_combine_kernel, dn_alpha=dn_alpha), grid=(n,),
        in_specs=[pl.BlockSpec((1, 1, ROWS_PER_STEP), lambda i: (i, 0, 0), memory_space=pltpu.SMEM),
                  pl.BlockSpec((1, 1, ROWS_PER_STEP), lambda i: (jnp.minimum(i + 1, n - 1), 0, 0),
                               memory_space=pltpu.SMEM),
                  pl.BlockSpec((TOK_BLOCK, LANE), lambda i: (i, 0)), tok, vec, vec,
                  pl.BlockSpec(memory_space=pl.ANY)],
        out_specs=tok,
        out_shape=jax.ShapeDtypeStruct((t, D_MODEL), F32),
        scratch_shapes=[pltpu.VMEM((2, TOP_K, TOK_BLOCK, D_MODEL), F32), pltpu.SemaphoreType.DMA((2,))],
        compiler_params=_cparams("arbitrary"), name="moe_combine_ln",
    )(dest3, dest3, gates, x1, g, b, ys)


def _onehots(idx):
    lane = lax.broadcasted_iota(jnp.int32, idx.shape, 1)
    return lane, [lane == idx[:, k:k + 1] for k in range(TOP_K)]


def _lanes(cols, lane):
    out = jnp.zeros(lane.shape, jnp.int32)
    for k, c in enumerate(cols):
        out = jnp.where(lane == k, c, out)
    return out


def _rank_kernel(cnt0_ref, idx_ref, rank_ref, cnt_ref, carry_ref):
    @pl.when(pl.program_id(0) == 0)
    def _():
        carry_ref[...] = cnt0_ref[...]

    n = TOK_BLOCK
    r = lax.broadcasted_iota(jnp.int32, (n, n), 0)
    c = lax.broadcasted_iota(jnp.int32, (n, n), 1)
    earlier = jnp.where(c < r, 1.0, 0.0).astype(BF16)
    carry = carry_ref[...]
    for sub in range(idx_ref.shape[0] // n):
        rows = slice(sub * n, (sub + 1) * n)
        lane, hots = _onehots(idx_ref[rows, :])
        chose = sum(h.astype(F32) for h in hots)
        before = _dot(earlier, chose.astype(BF16)) + carry
        ranks = [jnp.sum(jnp.where(h, before, 0.0), axis=-1, keepdims=True).astype(jnp.int32) for h in hots]
        rank_ref[rows, :] = _lanes(ranks, lane)
        carry = carry + jnp.sum(chose, axis=0, keepdims=True)
    carry_ref[...] = carry
    cnt_ref[...] = carry


def _route_block(t):
    return 1024 if t % 1024 == 0 else TOK_BLOCK


def _rank(cnt0, idx):
    t = idx.shape[0]
    tb = _route_block(t)
    tok = pl.BlockSpec((tb, LANE), lambda i: (i, 0))
    one = pl.BlockSpec((1, LANE), lambda i: (0, 0))
    return pl.pallas_call(
        _rank_kernel, grid=(t // tb,), in_specs=[one, tok], out_specs=[tok, one],
        out_shape=[jax.ShapeDtypeStruct((t, LANE), jnp.int32), jax.ShapeDtypeStruct((1, LANE), F32)],
        scratch_shapes=[pltpu.VMEM((1, LANE), F32)],
        compiler_params=_cparams("arbitrary"), name="moe_rank",
    )(cnt0, idx)


def _dest_kernel(cnt_ref, idx_ref, rank_ref, dest_ref, blk_ref, used_ref, start_ref):
    @pl.when(pl.program_id(0) == 0)
    def _():
        lane = lax.broadcasted_iota(jnp.int32, (8, LANE), 1)
        cnt = jnp.broadcast_to(cnt_ref[...], (8, LANE)).astype(jnp.int32)
        nblk = jnp.where(lane < N_EXPERTS, jnp.right_shift(cnt + (MOE_TM - 1), MOE_TM.bit_length() - 1), 0)
        end = nblk
        for s in (1, 2, 4, 8, 16):
            end = end + jnp.where(lane >= s, pltpu.roll(end, s, 1), 0)
        start_ref[...] = ((end - nblk) * MOE_TM).astype(F32)
        used_ref[...] = jnp.broadcast_to(end[:, N_EXPERTS - 1:N_EXPERTS], (8, LANE))
        blk = lax.broadcasted_iota(jnp.int32, blk_ref.shape, 0)
        lane_b = lax.broadcasted_iota(jnp.int32, blk_ref.shape, 1)
        done = jnp.where(lane_b < N_EXPERTS, jnp.where(end[0:1, :] <= blk, 1.0, 0.0), 0.0)
        blk_ref[...] = jnp.broadcast_to(
            jnp.minimum(jnp.sum(done, axis=-1, keepdims=True), N_EXPERTS - 1.0).astype(jnp.int32), blk_ref.shape)

    lane, hots = _onehots(idx_ref[...])
    start = start_ref[0:1, :]
    rank = rank_ref[...]
    dest_ref[...] = _lanes([jnp.sum(jnp.where(h, start, 0.0), axis=-1, keepdims=True).astype(jnp.int32)
                            + rank[:, k:k + 1] for k, h in enumerate(hots)], lane)


def _dest(cnt, idx, rank, n_blocks):
    t = idx.shape[0]
    tb = _route_block(t)
    tok = pl.BlockSpec((tb, LANE), lambda i: (i, 0))
    rows = -(-n_blocks // 8) * 8
    return pl.pallas_call(
        _dest_kernel, grid=(t // tb,),
        in_specs=[pl.BlockSpec((1, LANE), lambda i: (0, 0)), tok, tok],
        out_specs=[tok, pl.BlockSpec((rows, LANE), lambda i: (0, 0)), pl.BlockSpec((8, LANE), lambda i: (0, 0))],
        out_shape=[jax.ShapeDtypeStruct((t, LANE), jnp.int32), jax.ShapeDtypeStruct((rows, LANE), jnp.int32),
                   jax.ShapeDtypeStruct((8, LANE), jnp.int32)],
        scratch_shapes=[pltpu.VMEM((8, LANE), F32)],
        compiler_params=_cparams("arbitrary"), name="moe_dest",
    )(cnt, idx, rank)


def _moe(groups, w_gu, b_gu, w_dn, b_dn, g, b, dn_alpha):
    n_assign = sum(x1.shape[0] for x1, _, _ in groups) * TOP_K
    n_blocks = n_assign // MOE_TM + N_EXPERTS
    cnt = jnp.zeros((1, LANE), F32)
    ranks = []
    for _, idx, _ in groups:
        rank, cnt = _rank(cnt, idx)
        ranks.append(rank)
    xs = jnp.zeros((n_blocks * MOE_TM, D_MODEL), F32)
    dests = []
    for (x1, idx, _), rank in zip(groups, ranks):
        dest, blk, used = _dest(cnt, idx, rank, n_blocks)
        dest3 = dest[:, :TOP_K].reshape(-1, 1, TOK_BLOCK * TOP_K)
        xs = _scatter_rows(dest3, x1, xs)
        dests.append(dest3)
    ys = _experts(blk[:n_blocks, 0], used[0, :1], xs, w_gu, b_gu, w_dn, b_dn)
    return [_combine(dest3, gates, x1, g, b, ys, dn_alpha) for (x1, _, gates), dest3 in zip(groups, dests)]


def _project_all(x, w_in, pos, tm_plain, tm_rope, keys_t=None, operand_dtype=F32):
    n_groups = COL // LANE
    ret_rope = _rope_table(pos, R_DK) + (R_DK, (1.0,) * R_HEADS + (R_DK ** -0.5,) * R_HEADS)
    diff_tab = _rope_table(pos, D_DH)
    diff_rope = diff_tab + (D_DH, (1.0,) * n_groups)
    qk = _proj(x, w_in, 0, COL, tm_rope, ret_rope)
    vr = _proj(x, w_in, 1 * COL, COL, tm_plain, out_dtype=operand_dtype)
    gr = _proj(x, w_in, 2 * COL, COL, tm_plain)
    qd = _proj(x, w_in, 3 * COL, COL, tm_rope, diff_rope, out_dtype=operand_dtype)
    if keys_t is None:
        kd = _proj(x, w_in, 4 * COL, COL, tm_rope, diff_rope)
    else:
        cos_t, sin_t = (t[:, :D_DH].T for t in diff_tab)
        kd = _proj_keys_t(x, w_in, 4 * COL, tm_rope, cos_t, sin_t, *keys_t)
    vd = _proj(x, w_in, 5 * COL, COL, tm_plain)
    ga = _proj(x, w_in, 6 * COL, COL, tm_plain)
    gb = _proj(x, w_in, 7 * COL, COL, tm_plain)
    return qk, vr, gr, qd, kd, vd, ga, gb


def kernel(x_prompt, x_sample, cache_k, cache_v, state_ret, page_table, w_in, w_branch_ret, w_branch_diff, w_out, lam_q1, lam_k1, lam_q2, lam_k2, subln_w, ln1_g, ln1_b, w_router, b_router, w_gate_up, b_gate_up, w_down, b_down, ln2_g, ln2_b):
    depth = w_in.shape[0]
    assert depth == 1, "single-layer trunk"
    batch, seq, _ = x_prompt.shape
    dbatch, dseq, _ = x_sample.shape
    dn_alpha = (2.0 * depth) ** 0.25
    lam_init = 0.8 - 0.6 * math.exp(-0.3 * 0)
    n_p, n_s = batch * seq, dbatch * dseq

    w_in_b = w_in[0].astype(BF16)
    wr_b, wd_b, wo_b = w_branch_ret[0].astype(BF16), w_branch_diff[0].astype(BF16), w_out[0].astype(BF16)
    w_rt_b = w_router[0].astype(BF16)
    lam_args = (lam_q1, lam_k1, lam_q2, lam_k2, subln_w)

    xp = x_prompt.reshape(n_p, D_MODEL)
    qk, vr, gr, qd, kd_t, vd, ga, gb = _project_all(xp.astype(BF16), w_in_b, jnp.arange(seq), 1024, 512,
                                                    (batch, seq), BF16)
    o_r, s_p = _retention_prompt(qk, vr, gr, batch, seq)
    o_d = _dattn_prompt(lam_args, qd, kd_t, vd, batch, seq, lam_init)
    x1_p, idx_p, gate_p = _mix(xp, o_r, o_d, ga, gb, wr_b, wd_b, wo_b, ln1_g, ln1_b, w_rt_b, b_router, 512, dn_alpha)

    xs_ = x_sample.reshape(n_s, D_MODEL)
    pos_s = jnp.tile(PAST_LEN + jnp.arange(dseq), dbatch)
    qk_s, vr_s, gr_s, qd_s, kd_s, vd_s, ga_s, gb_s = _project_all(xs_, w_in_b, pos_s, n_s, n_s)
    pad8 = lambda a: jnp.pad(a.reshape(dbatch, dseq, -1), ((0, 0), (0, 8 - dseq), (0, 0)))
    pad_page = lambda a: jnp.pad(a.reshape(dbatch, dseq, -1), ((0, 0), (0, PAGE_SIZE - dseq), (0, 0)))
    o_r_s, s_s = _retention_sample(pad8(qk_s), pad8(vr_s), pad8(gr_s), state_ret[0], dseq)
    n_pool = cache_k.shape[1]
    cache_kt = jnp.transpose(cache_k[0], (0, 2, 3, 4, 1)).reshape(n_pool, D_HEADS * 2 * D_DH, PAGE_SIZE)
    cache_vr = cache_v[0].reshape(n_pool, PAGE_SIZE * D_HEADS, D_DV)
    kn_t = jnp.transpose(pad_page(kd_s), (0, 2, 1))
    vn = pad_page(vd_s).reshape(dbatch, PAGE_SIZE * D_HEADS, D_DV)
    o_d_s = _dattn_decode(lam_args, page_table, pad8(qd_s), kn_t, vn, cache_kt, cache_vr, lam_init)
    o_r_s = o_r_s[:, :dseq].reshape(n_s, -1)
    o_d_s = o_d_s[:, :dseq].reshape(n_s, -1)
    x1_s, idx_s, gate_s = _mix(xs_, o_r_s, o_d_s, ga_s, gb_s, wr_b, wd_b, wo_b, ln1_g, ln1_b, w_rt_b, b_router,
                               n_s, dn_alpha)

    y_p, y_s = _moe([(x1_p, idx_p, gate_p), (x1_s, idx_s, gate_s)],
                    w_gate_up[0], b_gate_up[0][:, None, :], w_down[0], b_down[0][:, None, :], ln2_g, ln2_b, dn_alpha)

    return (y_p.reshape(batch, seq, D_MODEL),
            y_s.reshape(dbatch, dseq, D_MODEL),
            jnp.transpose(kd_t.reshape(batch, D_HEADS, 2, D_DH, seq), (0, 4, 1, 2, 3))[None],
            vd.reshape(1, batch, seq, D_HEADS, D_DV),
            s_p[None],
            kd_s.reshape(1, dbatch, dseq, D_HEADS, 2, D_DH),
            vd_s.reshape(1, dbatch, dseq, D_HEADS, D_DV),
            s_s[None])
```

```python
import functools
import math

import jax
import jax.numpy as jnp
from jax import lax
from jax.experimental import pallas as pl
from jax.experimental.pallas import tpu as pltpu

F32 = jnp.float32
BF16 = jnp.bfloat16

D_MODEL = 1024
PAST_LEN = 16384
PAGE_SIZE = 128
R_HEADS, R_DK, R_DV, R_CHUNK = 4, 128, 256, 128
D_HEADS, D_DH, D_DV = 8, 64, 128
ROPE_THETA = 10000.0
N_EXPERTS, TOP_K, D_FF = 32, 4, 1024
SWIGLU_ALPHA, SWIGLU_LIMIT = 1.702, 7.0
NEG_INF = -1e30

LANE = 128
VMEM_LIMIT = 56 * 1024 * 1024
ATTN_BLOCK = 512
DEC_PAGES = 16
MOE_TM = 512
TOK_BLOCK = 128
COL = 1024


def _cparams(*sem):
    return pltpu.CompilerParams(dimension_semantics=sem, vmem_limit_bytes=VMEM_LIMIT)


def _dot(a, b):
    return jnp.dot(a, b, preferred_element_type=F32)


def _dot_nt(a, b):
    return lax.dot_general(a, b, (((1,), (1,)), ((), ())), preferred_element_type=F32)


def _swap_halves(sl, unit):
    if unit == LANE:
        return pltpu.roll(sl, LANE // 2, 1)
    lane = lax.broadcasted_iota(jnp.int32, sl.shape, 1)
    fwd = pltpu.roll(sl, unit // 2, 1)
    bwd = pltpu.roll(sl, LANE - unit // 2, 1)
    return jnp.where((lane % unit) < unit // 2, bwd, fwd)


def _proj_kernel(x_ref, w_ref, o_ref):
    o_ref[...] = _dot(x_ref[...].astype(BF16), w_ref[...]).astype(o_ref.dtype)


def _proj_rope_kernel(x_ref, w_ref, cos_ref, sin_ref, o_ref, *, unit, scales):
    h = _dot(x_ref[...].astype(BF16), w_ref[...])
    cos, sin = cos_ref[...], sin_ref[...]
    for g, scale in enumerate(scales):
        sl = h[:, g * LANE:(g + 1) * LANE]
        o = sl * cos + _swap_halves(sl, unit) * sin
        if scale != 1.0:
            o = o * scale
        o_ref[:, g * LANE:(g + 1) * LANE] = o.astype(o_ref.dtype)


def _proj_rope_t_kernel(x_ref, w_ref, cos_ref, sin_ref, o_ref):
    ht = _dot(x_ref[...].astype(BF16), w_ref[...]).T
    half = D_DH // 2
    cos, sin = cos_ref[...], sin_ref[...]
    parts = []
    for u in range(ht.shape[0] // D_DH):
        blk = ht[u * D_DH:(u + 1) * D_DH]
        swapped = jnp.concatenate([blk[half:], blk[:half]], axis=0)
        parts.append(blk * cos + swapped * sin)
    o_ref[...] = jnp.concatenate(parts, axis=0)


def _proj_keys_t(x, w, col0, tm, cos_t, sin_t, batch, seq):
    n_pos = seq // tm
    return pl.pallas_call(
        _proj_rope_t_kernel, grid=(batch * n_pos,),
        in_specs=[pl.BlockSpec((tm, D_MODEL), lambda i: (i, 0)),
                  pl.BlockSpec((D_MODEL, COL), lambda i: (0, col0 // COL)),
                  pl.BlockSpec((D_DH, tm), lambda i: (0, i % n_pos)),
                  pl.BlockSpec((D_DH, tm), lambda i: (0, i % n_pos))],
        out_specs=pl.BlockSpec((COL, tm), lambda i: (i // n_pos, i % n_pos)),
        out_shape=jax.ShapeDtypeStruct((batch * COL, seq), F32),
        compiler_params=_cparams("parallel"), name="proj_keys_t",
    )(x, w, cos_t, sin_t)


def _proj(x, w, col0, ncols, tm, rope=None, out_dtype=F32):
    t = x.shape[0]
    grid = (t // tm, ncols // COL)
    in_specs = [pl.BlockSpec((tm, D_MODEL), lambda i, j: (i, 0)),
                pl.BlockSpec((D_MODEL, COL), lambda i, j: (0, col0 // COL + j))]
    args = [x, w]
    if rope is None:
        body = _proj_kernel
    else:
        cos, sin, unit, scales = rope
        n_pos = cos.shape[0] // tm
        in_specs += [pl.BlockSpec((tm, LANE), lambda i, j: (i % n_pos, 0))] * 2
        args += [cos, sin]
        body = functools.partial(_proj_rope_kernel, unit=unit, scales=scales)
    return pl.pallas_call(
        body, grid=grid, in_specs=in_specs,
        out_specs=pl.BlockSpec((tm, COL), lambda i, j: (i, j)),
        out_shape=jax.ShapeDtypeStruct((t, ncols), out_dtype),
        compiler_params=_cparams("parallel", "arbitrary"), name="proj" if rope is None else "proj_rope",
    )(*args)


def _rope_table(pos, unit):
    half = unit // 2
    inv = ROPE_THETA ** (-jnp.arange(half, dtype=F32) / half)
    ang = pos.astype(F32)[:, None] * inv[None, :]
    cos, sin = jnp.cos(ang), jnp.sin(ang)
    c = jnp.concatenate([cos, cos], axis=-1)
    s = jnp.concatenate([-sin, sin], axis=-1)
    return jnp.tile(c, (1, LANE // unit)), jnp.tile(s, (1, LANE // unit))


def _ret_tables(chunk, n_tok):
    log_gamma = jnp.log(1.0 - 2.0 ** (-5.0 - jnp.arange(R_HEADS, dtype=F32)))
    idx = jnp.arange(chunk, dtype=F32)
    diff = idx[:, None] - idx[None, :]
    causal = diff >= 0
    dec = jnp.where(causal[None], jnp.exp(log_gamma[:, None, None] * jnp.where(causal, diff, 0.0)[None]), 0.0)
    qdec = jnp.exp(log_gamma[:, None] * (idx[None, :] + 1.0))
    kdec = jnp.exp(log_gamma[:, None] * (n_tok - 1.0 - idx[None, :]))
    sdec = jnp.exp(log_gamma * n_tok)
    return (dec,
            jnp.broadcast_to(qdec[:, :, None], (R_HEADS, chunk, R_DV)),
            jnp.broadcast_to(kdec[:, :, None], (R_HEADS, chunk, R_DK)),
            jnp.broadcast_to(sdec[:, None, None], (R_HEADS, 1, R_DV)))


def _ret_chunk(q, k, v, g, state, dec, qdec, kdec, sdec):
    qb, kb, vb = q.astype(BF16), k.astype(BF16), v.astype(BF16)
    scores = _dot_nt(qb, kb) * dec
    inner = _dot(scores.astype(BF16), vb)
    cross = _dot(qb, state.astype(BF16)) * qdec
    o = inner + cross
    kd_t = (k * kdec).T.astype(BF16)
    s_new = state * sdec + _dot(kd_t, vb)
    mu = jnp.mean(o, axis=-1, keepdims=True)
    var = jnp.mean(jnp.square(o - mu), axis=-1, keepdims=True)
    o = (o - mu) * lax.rsqrt(var + 1e-6)
    return o * (g * jax.nn.sigmoid(g)), s_new


def _ret_prompt_kernel(q_ref, k_ref, v_ref, g_ref, dec_ref, qdec_ref, kdec_ref, sdec_ref, o_ref, s_ref):
    @pl.when(pl.program_id(1) == 0)
    def _():
        s_ref[...] = jnp.zeros_like(s_ref)

    for b in range(RET_SEQS):
        for h in range(R_HEADS):
            dk, dv = slice(h * R_DK, (h + 1) * R_DK), slice(h * R_DV, (h + 1) * R_DV)
            o, s_new = _ret_chunk(q_ref[b, :, dk], k_ref[b, :, dk], v_ref[b, :, dv], g_ref[b, :, dv], s_ref[b, h],
                                  dec_ref[h], qdec_ref[h], kdec_ref[h], sdec_ref[h])
            o_ref[b, :, dv] = o.astype(o_ref.dtype)
            s_ref[b, h] = s_new


RET_SEQS = 2


def _retention_prompt(qk, vr, gr, batch, seq):
    nc = seq // R_CHUNK
    tabs = _ret_tables(R_CHUNK, R_CHUNK)
    whole = lambda a: pl.BlockSpec(a.shape, lambda b, c: (0,) * a.ndim)
    seq3 = lambda a: a.reshape(batch, seq, a.shape[-1])
    o, s = pl.pallas_call(
        _ret_prompt_kernel, grid=(batch // RET_SEQS, nc),
        in_specs=[
            pl.BlockSpec((RET_SEQS, R_CHUNK, R_HEADS * R_DK), lambda b, c: (b, c, 0)),
            pl.BlockSpec((RET_SEQS, R_CHUNK, R_HEADS * R_DK), lambda b, c: (b, c, 1)),
            pl.BlockSpec((RET_SEQS, R_CHUNK, R_HEADS * R_DV), lambda b, c: (b, c, 0)),
            pl.BlockSpec((RET_SEQS, R_CHUNK, R_HEADS * R_DV), lambda b, c: (b, c, 0)),
        ] + [whole(a) for a in tabs],
        out_specs=[
            pl.BlockSpec((RET_SEQS, R_CHUNK, R_HEADS * R_DV), lambda b, c: (b, c, 0)),
            pl.BlockSpec((RET_SEQS, R_HEADS, R_DK, R_DV), lambda b, c: (b, 0, 0, 0)),
        ],
        out_shape=[jax.ShapeDtypeStruct((batch, seq, R_HEADS * R_DV), BF16),
                   jax.ShapeDtypeStruct((batch, R_HEADS, R_DK, R_DV), F32)],
        compiler_params=_cparams("parallel", "arbitrary"), name="retention_prompt",
    )(seq3(qk), seq3(qk), seq3(vr), seq3(gr), *tabs)
    return o.reshape(batch * seq, -1), s


def _ret_sample_kernel(q_ref, k_ref, v_ref, g_ref, s0_ref, dec_ref, qdec_ref, kdec_ref, sdec_ref,
                       o_ref, s_ref, qp, kp, vp, gp):
    n = q_ref.shape[1]
    for pad, src in ((qp, q_ref), (kp, k_ref), (vp, v_ref), (gp, g_ref)):
        pad[...] = jnp.zeros_like(pad)
        pad[0:n, :] = src[0]
    o, s_new = _ret_chunk(qp[...], kp[...], vp[...], gp[...], s0_ref[0, 0],
                          dec_ref[0], qdec_ref[0], kdec_ref[0], sdec_ref[0])
    o_ref[0] = o[0:n, :]
    s_ref[0, 0] = s_new


def _retention_sample(qk, vr, gr, state, n_tok):
    batch, rows, _ = qk.shape
    tabs = _ret_tables(R_CHUNK, n_tok)
    return pl.pallas_call(
        _ret_sample_kernel, grid=(batch, R_HEADS),
        in_specs=[
            pl.BlockSpec((1, rows, R_DK), lambda b, h: (b, 0, h)),
            pl.BlockSpec((1, rows, R_DK), lambda b, h: (b, 0, R_HEADS + h)),
            pl.BlockSpec((1, rows, R_DV), lambda b, h: (b, 0, h)),
            pl.BlockSpec((1, rows, R_DV), lambda b, h: (b, 0, h)),
            pl.BlockSpec((1, 1, R_DK, R_DV), lambda b, h: (b, h, 0, 0)),
            pl.BlockSpec((1, R_CHUNK, R_CHUNK), lambda b, h: (h, 0, 0)),
            pl.BlockSpec((1, R_CHUNK, R_DV), lambda b, h: (h, 0, 0)),
            pl.BlockSpec((1, R_CHUNK, R_DK), lambda b, h: (h, 0, 0)),
            pl.BlockSpec((1, 1, R_DV), lambda b, h: (h, 0, 0)),
        ],
        out_specs=[
            pl.BlockSpec((1, rows, R_DV), lambda b, h: (b, 0, h)),
            pl.BlockSpec((1, 1, R_DK, R_DV), lambda b, h: (b, h, 0, 0)),
        ],
        out_shape=[jax.ShapeDtypeStruct((batch, rows, R_HEADS * R_DV), F32),
                   jax.ShapeDtypeStruct((batch, R_HEADS, R_DK, R_DV), F32)],
        scratch_shapes=[pltpu.VMEM((R_CHUNK, R_DK), F32), pltpu.VMEM((R_CHUNK, R_DK), F32),
                        pltpu.VMEM((R_CHUNK, R_DV), F32), pltpu.VMEM((R_CHUNK, R_DV), F32)],
        compiler_params=_cparams("parallel", "parallel"), name="retention_sample",
    )(qk, qk, vr, gr, state, *tabs)


def _lambda(lq1_ref, lk1_ref, lq2_ref, lk2_ref, lam_init):
    a = jnp.sum(lq1_ref[...] * lk1_ref[...], axis=-1, keepdims=True)
    b = jnp.sum(lq2_ref[...] * lk2_ref[...], axis=-1, keepdims=True)
    return jnp.exp(a) - jnp.exp(b) + lam_init


def _sub_rms(o, w, lam_init):
    ms = jnp.mean(jnp.square(o), axis=-1, keepdims=True)
    return o * lax.rsqrt(ms + 1e-5) * w * (1.0 - lam_init)


def _dattn_prompt_kernel(lq1_ref, lk1_ref, lq2_ref, lk2_ref, w_ref, q_ref, kt_ref, v_ref, o_ref,
                         kc_ref, vx_ref, *, blk, lam_init):
    kc_ref[...] = kt_ref[...].astype(BF16)
    vx_ref[:, 0:D_DV] = v_ref[...].astype(BF16)
    vx_ref[:, D_DV:2 * D_DV] = jnp.ones((vx_ref.shape[0], D_DV), BF16)
    row = lax.broadcasted_iota(jnp.int32, (blk, blk), 0)
    col = lax.broadcasted_iota(jnp.int32, (blk, blk), 1)
    lam = _lambda(lq1_ref, lk1_ref, lq2_ref, lk2_ref, lam_init)

    for qi in range(q_ref.shape[0] // blk):
        q = q_ref[qi * blk:(qi + 1) * blk, :] * (D_DH ** -0.5)
        outs = []
        for c in range(2):
            qc = q[:, c * D_DH:(c + 1) * D_DH].astype(BF16)
            m = jnp.full((blk, 1), NEG_INF, F32)
            acc = jnp.zeros((blk, 2 * D_DV), F32)
            for j in range(qi + 1):
                s = _dot(qc, kc_ref[c * D_DH:(c + 1) * D_DH, j * blk:(j + 1) * blk])
                if j == qi:
                    s = jnp.where(col <= row, s, NEG_INF)
                m_new = jnp.maximum(m, jnp.max(s, axis=-1, keepdims=True))
                p = jnp.exp(s - m_new)
                acc = acc * jnp.exp(m - m_new) + _dot(p.astype(BF16), vx_ref[j * blk:(j + 1) * blk, :])
                m = m_new
            outs.append(acc[:, 0:D_DV] / acc[:, D_DV:D_DV + 1])
        o_ref[qi * blk:(qi + 1) * blk, :] = _sub_rms(outs[0] - lam * outs[1], w_ref[...],
                                                     lam_init).astype(o_ref.dtype)


def _lam_specs(n):
    zero = lambda *_: (0, 0)
    return [pl.BlockSpec((1, D_DH), zero)] * 4 + [pl.BlockSpec((1, D_DV), zero)]


def _dattn_prompt(lam_args, qd, kd_t, vd, batch, seq, lam_init):
    blk = min(ATTN_BLOCK, seq)
    return pl.pallas_call(
        functools.partial(_dattn_prompt_kernel, blk=blk, lam_init=lam_init),
        grid=(batch, D_HEADS),
        in_specs=_lam_specs(2) + [
            pl.BlockSpec((seq, 2 * D_DH), lambda b, h: (b, h)),
            pl.BlockSpec((2 * D_DH, seq), lambda b, h: (b * D_HEADS + h, 0)),
            pl.BlockSpec((seq, D_DV), lambda b, h: (b, h)),
        ],
        out_specs=pl.BlockSpec((seq, D_DV), lambda b, h: (b, h)),
        out_shape=jax.ShapeDtypeStruct((batch * seq, D_HEADS * D_DV), BF16),
        scratch_shapes=[pltpu.VMEM((2 * D_DH, seq), BF16), pltpu.VMEM((seq, 2 * D_DV), BF16)],
        compiler_params=_cparams("parallel", "parallel"), name="dattn_prompt",
    )(*lam_args, qd, kd_t, vd)


DEC_ROWS = D_HEADS * 2 * 8


def _dattn_decode_kernel(pt_ref, lq1_ref, lk1_ref, lq2_ref, lk2_ref, w_ref, q_ref, kn_ref, vn_ref, ck_ref, cv_ref,
                         o_ref, qbd_ref, m_ref, l_ref, acc_ref, kbuf, vbuf, sems, *, n_pages, lam_init):
    j = pl.program_id(1)
    step = pl.program_id(0) * pl.num_programs(1) + j
    slot = step % 2

    def fetch(s, into):
        for p in range(n_pages):
            page = pt_ref[s * n_pages + p]
            pltpu.make_async_copy(ck_ref.at[page], kbuf.at[into, p], sems.at[0, into]).start()
            pltpu.make_async_copy(cv_ref.at[page], vbuf.at[into, p], sems.at[1, into]).start()

    @pl.when(step == 0)
    def _():
        fetch(0, 0)

    @pl.when(step + 1 < pl.num_programs(0) * pl.num_programs(1))
    def _():
        fetch(step + 1, 1 - slot)

    pltpu.make_async_copy(ck_ref.at[pl.ds(0, n_pages)], kbuf.at[slot], sems.at[0, slot]).wait()
    pltpu.make_async_copy(cv_ref.at[pl.ds(0, n_pages)], vbuf.at[slot], sems.at[1, slot]).wait()

    @pl.when(j == 0)
    def _():
        q8 = q_ref[0] * (D_DH ** -0.5)
        tiled = jnp.concatenate([q8] * (DEC_ROWS // 8), axis=0)
        r = lax.broadcasted_iota(jnp.int32, tiled.shape, 0)
        cidx = lax.broadcasted_iota(jnp.int32, tiled.shape, 1)
        qbd_ref[...] = jnp.where(r // 8 == cidx // D_DH, tiled, 0.0).astype(BF16)
        m_ref[...] = jnp.full(m_ref.shape, NEG_INF, F32)
        l_ref[...] = jnp.zeros(l_ref.shape, F32)
        acc_ref[...] = jnp.zeros(acc_ref.shape, F32)

    rows_h = DEC_ROWS // D_HEADS

    def update(s_list, v_pages):
        m = m_ref[...]
        m_new = m
        for s in s_list:
            m_new = jnp.maximum(m_new, jnp.max(s, axis=-1, keepdims=True))
        alpha = jnp.exp(m - m_new)
        p = jnp.concatenate([jnp.exp(s - m_new) for s in s_list], axis=1)
        l_ref[...] = l_ref[...] * alpha + jnp.sum(p, axis=-1, keepdims=True)
        m_ref[...] = m_new
        p = p.astype(BF16)
        for h in range(D_HEADS):
            rows = slice(h * rows_h, (h + 1) * rows_h)
            v_h = jnp.concatenate([v[pl.ds(h, PAGE_SIZE, stride=D_HEADS), :] for v in v_pages], axis=0)
            acc_ref[rows, :] = acc_ref[rows, :] * alpha[rows, :] + _dot(p[rows, :], v_h.astype(BF16))

    qbd = qbd_ref[...]
    update([_dot(qbd, kbuf[slot, p].astype(BF16)) for p in range(n_pages)],
           [vbuf.at[slot, p] for p in range(n_pages)])

    @pl.when(j == pl.num_programs(1) - 1)
    def _():
        s = _dot(qbd, kn_ref[0].astype(BF16))
        t_q = lax.broadcasted_iota(jnp.int32, s.shape, 0) % 8
        t_k = lax.broadcasted_iota(jnp.int32, s.shape, 1)
        update([jnp.where(t_k <= t_q, s, NEG_INF)], [vn_ref.at[0]])
        acc = acc_ref[...] / l_ref[...]
        lam = _lambda(lq1_ref, lk1_ref, lq2_ref, lk2_ref, lam_init)
        w = w_ref[...]
        heads = []
        for h in range(D_HEADS):
            o = acc[h * rows_h:h * rows_h + 8, :] - lam * acc[h * rows_h + 8:(h + 1) * rows_h, :]
            heads.append(_sub_rms(o, w, lam_init))
        o_ref[0] = jnp.concatenate(heads, axis=1)


def _dattn_decode(lam_args, page_table, q8, kn_t, vn, cache_kt, cache_v, lam_init):
    batch, n_pages = page_table.shape
    steps = n_pages // DEC_PAGES
    width = q8.shape[-1]
    page = (1,) + cache_kt.shape[1:]
    pages = (2, DEC_PAGES) + cache_kt.shape[1:]
    zero = lambda b, j, pt: (0, 0)
    seq_spec = lambda shape: pl.BlockSpec(shape, lambda b, j, pt: (b, 0, 0))
    grid_spec = pltpu.PrefetchScalarGridSpec(
        num_scalar_prefetch=1, grid=(batch, steps),
        in_specs=([pl.BlockSpec((1, D_DH), zero)] * 4 + [pl.BlockSpec((1, D_DV), zero)]
                  + [seq_spec((1, 8, width)), seq_spec(page), seq_spec(page)]
                  + [pl.BlockSpec(memory_space=pl.ANY)] * 2),
        out_specs=seq_spec((1, 8, width)),
        scratch_shapes=[pltpu.VMEM((DEC_ROWS, width), BF16), pltpu.VMEM((DEC_ROWS, 1), F32),
                        pltpu.VMEM((DEC_ROWS, 1), F32), pltpu.VMEM((DEC_ROWS, D_DV), F32),
                        pltpu.VMEM(pages, F32), pltpu.VMEM(pages, F32), pltpu.SemaphoreType.DMA((2, 2))])
    return pl.pallas_call(
        functools.partial(_dattn_decode_kernel, n_pages=DEC_PAGES, lam_init=lam_init),
        grid_spec=grid_spec,
        out_shape=jax.ShapeDtypeStruct((batch, 8, width), F32),
        compiler_params=_cparams("arbitrary", "arbitrary"), name="dattn_decode",
    )(page_table.reshape(-1), *lam_args, q8, kn_t, vn, cache_kt, cache_v)


def _layer_norm(y, g, b):
    mu = jnp.mean(y, axis=-1, keepdims=True)
    var = jnp.mean(jnp.square(y - mu), axis=-1, keepdims=True)
    return (y - mu) * lax.rsqrt(var + 1e-5) * g + b


def _mix_kernel(x_ref, or_ref, od_ref, ga_ref, gb_ref, wr_ref, wd_ref, wo_ref, g_ref, b_ref, wrt_ref, brt_ref,
                x1_ref, idx_ref, gate_ref, *, dn_alpha):
    r = _dot(or_ref[...].astype(BF16), wr_ref[...])
    d = _dot(od_ref[...].astype(BF16), wd_ref[...])
    mix = jax.nn.sigmoid(ga_ref[...]) * r + jax.nn.sigmoid(gb_ref[...]) * d
    y = dn_alpha * x_ref[...] + _dot(mix.astype(BF16), wo_ref[...])
    x1 = _layer_norm(y, g_ref[...], b_ref[...])
    x1_ref[...] = x1
    vals = _dot(x1.astype(BF16), wrt_ref[...]) + brt_ref[...]
    col = lax.broadcasted_iota(jnp.int32, vals.shape, 1)
    lane = lax.broadcasted_iota(jnp.int32, idx_ref.shape, 1)
    idx_out = jnp.zeros(idx_ref.shape, jnp.int32)
    val_out = jnp.full(gate_ref.shape, NEG_INF, F32)
    for k in range(TOP_K):
        mx = jnp.max(vals, axis=-1, keepdims=True)
        first = jnp.min(jnp.where(vals == mx, col, N_EXPERTS), axis=-1, keepdims=True)
        idx_out = jnp.where(lane == k, first, idx_out)
        val_out = jnp.where(lane == k, mx, val_out)
        vals = jnp.where(col == first, NEG_INF, vals)
    e = jnp.exp(val_out - jnp.max(val_out, axis=-1, keepdims=True))
    idx_ref[...] = idx_out
    gate_ref[...] = e / jnp.sum(e, axis=-1, keepdims=True)


def _mix(x, o_r, o_d, ga, gb, wr, wd, wo, g, b, w_rt, b_rt, tm, dn_alpha):
    t = x.shape[0]
    tok = pl.BlockSpec((tm, D_MODEL), lambda i: (i, 0))
    mat = pl.BlockSpec((D_MODEL, D_MODEL), lambda i: (0, 0))
    vec = pl.BlockSpec((1, D_MODEL), lambda i: (0, 0))
    return pl.pallas_call(
        functools.partial(_mix_kernel, dn_alpha=dn_alpha), grid=(t // tm,),
        in_specs=[tok] * 5 + [mat] * 3 + [vec, vec,
                                          pl.BlockSpec((D_MODEL, N_EXPERTS), lambda i: (0, 0)),
                                          pl.BlockSpec((1, N_EXPERTS), lambda i: (0, 0))],
        out_specs=[tok, pl.BlockSpec((tm, LANE), lambda i: (i, 0)), pl.BlockSpec((tm, LANE), lambda i: (i, 0))],
        out_shape=[jax.ShapeDtypeStruct((t, D_MODEL), F32), jax.ShapeDtypeStruct((t, LANE), jnp.int32),
                   jax.ShapeDtypeStruct((t, LANE), F32)],
        compiler_params=_cparams("parallel"), name="mix_ln_router",
    )(x, o_r, o_d, ga, gb, wr, wd, wo, g, b, w_rt, b_rt)


ROWS_PER_STEP = TOK_BLOCK * TOP_K


SCATTER_SLOTS = 3


def _zero_padding(pad_begin_ref, pad_len_ref, n_used_ref, xs_ref, zbuf, sem, first_tail, n_blocks):
    zbuf[...] = jnp.zeros_like(zbuf)
    bits = MOE_TM.bit_length() - 1

    def pieces():
        for e in range(N_EXPERTS):
            begin, length = pad_begin_ref[e], pad_len_ref[e]
            head = jnp.bitwise_and(-begin, 7)
            for k in range(7):
                yield k < head, begin + k, 1
            begin8, length8 = begin + head, length - head
            for bit in range(3, bits):
                higher = jnp.left_shift(jnp.right_shift(length8, bit + 1), bit + 1)
                yield (jnp.bitwise_and(jnp.right_shift(length8, bit), 1) == 1,
                       pl.multiple_of(begin8 + higher, 8), 1 << bit)
        for blk in range(first_tail, n_blocks):
            yield blk >= n_used_ref[0], blk * MOE_TM, MOE_TM

    for wait in (False, True):
        for cond, row, rows in pieces():
            @pl.when(cond)
            def _(row=row, rows=rows):
                copy = pltpu.make_async_copy(zbuf.at[pl.ds(0, rows)], xs_ref.at[pl.ds(row, rows)], sem)
                copy.wait() if wait else copy.start()


def _scatter_rows_kernel(pad_begin_ref, pad_len_ref, n_used_ref, dest_ref, xa_ref, xb_ref, xs_ref,
                         xbuf, zbuf, load_sems, row_sems, zero_sem, *, n_first, first_tail, n_blocks):
    i = pl.program_id(0)
    n = pl.num_programs(0)
    slot = i % SCATTER_SLOTS

    def load_start(step):
        s = step % SCATTER_SLOTS

        @pl.when(step < n_first)
        def _():
            rows = pl.ds(pl.multiple_of(step * TOK_BLOCK, TOK_BLOCK), TOK_BLOCK)
            pltpu.make_async_copy(xa_ref.at[rows], xbuf.at[s], load_sems.at[s]).start()

        @pl.when(step >= n_first)
        def _():
            rows = pl.ds(pl.multiple_of((step - n_first) * TOK_BLOCK, TOK_BLOCK), TOK_BLOCK)
            pltpu.make_async_copy(xb_ref.at[rows], xbuf.at[s], load_sems.at[s]).start()

    @pl.when(i == 0)
    def _():
        load_start(0)

        @pl.when(n > 1)
        def _():
            load_start(1)

        _zero_padding(pad_begin_ref, pad_len_ref, n_used_ref, xs_ref, zbuf, zero_sem, first_tail, n_blocks)

    pltpu.make_async_copy(xa_ref.at[pl.ds(0, TOK_BLOCK)], xbuf.at[slot], load_sems.at[slot]).wait()

    def start(t, c):
        for k in range(TOP_K):
            pltpu.make_async_copy(xbuf.at[slot, pl.ds(t, 1)], xs_ref.at[pl.ds(dest_ref[0, 0, t * TOP_K + k], 1)],
                                  row_sems.at[slot]).start(priority=k % 2)
        return c

    lax.fori_loop(0, TOK_BLOCK, start, 0, unroll=8)

    def drain(s):
        pltpu.make_async_copy(xs_ref.at[pl.ds(0, ROWS_PER_STEP)], xs_ref.at[pl.ds(ROWS_PER_STEP, ROWS_PER_STEP)],
                              row_sems.at[s]).wait()

    @pl.when(i > 0)
    def _():
        drain((i - 1) % SCATTER_SLOTS)

    @pl.when(i + 2 < n)
    def _():
        load_start(i + 2)

    @pl.when(i == n - 1)
    def _():
        drain(slot)


def _scatter_rows(pad_begin, pad_len, n_used, dest3, xa, xb, n_blocks):
    n_first = xa.shape[0] // TOK_BLOCK
    n_steps = n_first + xb.shape[0] // TOK_BLOCK
    first_tail = n_steps * ROWS_PER_STEP // MOE_TM
    grid_spec = pltpu.PrefetchScalarGridSpec(
        num_scalar_prefetch=3, grid=(n_steps,),
        in_specs=[pl.BlockSpec((1, 1, ROWS_PER_STEP), lambda i, *_: (i, 0, 0), memory_space=pltpu.SMEM),
                  pl.BlockSpec(memory_space=pl.ANY),
                  pl.BlockSpec(memory_space=pl.ANY)],
        out_specs=pl.BlockSpec(memory_space=pl.ANY),
        scratch_shapes=[pltpu.VMEM((SCATTER_SLOTS, TOK_BLOCK, D_MODEL), F32), pltpu.VMEM((MOE_TM, D_MODEL), F32),
                        pltpu.SemaphoreType.DMA((SCATTER_SLOTS,)), pltpu.SemaphoreType.DMA((SCATTER_SLOTS,)),
                        pltpu.SemaphoreType.DMA])
    return pl.pallas_call(
        functools.partial(_scatter_rows_kernel, n_first=n_first, first_tail=first_tail, n_blocks=n_blocks),
        grid_spec=grid_spec,
        out_shape=jax.ShapeDtypeStruct((n_blocks * MOE_TM, D_MODEL), F32),
        compiler_params=_cparams("arbitrary"), name="moe_scatter_rows",
    )(pad_begin, pad_len, n_used, dest3, xa, xb)


def _expert_kernel(blk_e_ref, n_used_ref, x_ref, wgu_ref, bgu_ref, wdn_ref, bdn_ref, y_ref, wgu_b, wdn_b):
    i = pl.program_id(0)
    used = i < n_used_ref[0]

    @pl.when(jnp.logical_not(used))
    def _():
        y_ref[...] = jnp.zeros_like(y_ref)

    first = jnp.logical_or(i == 0, blk_e_ref[i] != blk_e_ref[jnp.maximum(i - 1, 0)])

    @pl.when(jnp.logical_and(used, first))
    def _():
        wgu_b[...] = wgu_ref[0].astype(BF16)
        wdn_b[...] = wdn_ref[0].astype(BF16)

    @pl.when(used)
    def _():
        hgu = _dot(x_ref[...].astype(BF16), wgu_b[...]) + bgu_ref[0]
        glu = jnp.minimum(hgu[:, 0:D_FF], SWIGLU_LIMIT)
        lin = jnp.clip(hgu[:, D_FF:2 * D_FF], -SWIGLU_LIMIT, SWIGLU_LIMIT)
        act = glu * jax.nn.sigmoid(SWIGLU_ALPHA * glu) * (lin + 1.0)
        y_ref[...] = _dot(act.astype(BF16), wdn_b[...]) + bdn_ref[0]


def _experts(blk_e, n_used, xs, w_gu, b_gu, w_dn, b_dn):
    n_rows = xs.shape[0]
    blk = lambda i, be, nu: jnp.minimum(i, nu[0] - 1)
    exp = lambda i, be, nu: be[jnp.minimum(i, nu[0] - 1)]
    grid_spec = pltpu.PrefetchScalarGridSpec(
        num_scalar_prefetch=2, grid=(n_rows // MOE_TM,),
        in_specs=[pl.BlockSpec((MOE_TM, D_MODEL), lambda i, be, nu: (blk(i, be, nu), 0)),
                  pl.BlockSpec((1, D_MODEL, 2 * D_FF), lambda i, be, nu: (exp(i, be, nu), 0, 0)),
                  pl.BlockSpec((1, 1, 2 * D_FF), lambda i, be, nu: (exp(i, be, nu), 0, 0)),
                  pl.BlockSpec((1, D_FF, D_MODEL), lambda i, be, nu: (exp(i, be, nu), 0, 0)),
                  pl.BlockSpec((1, 1, D_MODEL), lambda i, be, nu: (exp(i, be, nu), 0, 0))],
        out_specs=pl.BlockSpec((MOE_TM, D_MODEL), lambda i, be, nu: (i, 0)),
        scratch_shapes=[pltpu.VMEM((D_MODEL, 2 * D_FF), BF16), pltpu.VMEM((D_FF, D_MODEL), BF16)])
    return pl.pallas_call(
        _expert_kernel, grid_spec=grid_spec,
        out_shape=jax.ShapeDtypeStruct((n_rows, D_MODEL), F32),
        compiler_params=_cparams("arbitrary"), name="moe_experts",
    )(blk_e, n_used, xs, w_gu, b_gu, w_dn, b_dn)


def _combine_kernel(dest_ref, next_ref, gate_ref, x_ref, g_ref, b_ref, ys_ref, o_ref, buf, sems, *, dn_alpha):
    i = pl.program_id(0)
    slot = i % 2

    def issue(d_ref, s):
        def start(t, c):
            for k in range(TOP_K):
                pltpu.make_async_copy(ys_ref.at[pl.ds(d_ref[0, 0, t * TOP_K + k], 1)],
                                      buf.at[s, k, pl.ds(t, 1)], sems.at[s]).start(priority=k % 2)
            return c

        lax.fori_loop(0, TOK_BLOCK, start, 0, unroll=8)

    @pl.when(i == 0)
    def _():
        issue(dest_ref, 0)

    @pl.when(i + 1 < pl.num_programs(0))
    def _():
        issue(next_ref, 1 - slot)

    for k in range(TOP_K):
        pltpu.make_async_copy(ys_ref.at[pl.ds(0, TOK_BLOCK)], buf.at[slot, k], sems.at[slot]).wait()
    gate = gate_ref[...]
    y = dn_alpha * x_ref[...]
    for k in range(TOP_K):
        y = y + gate[:, k:k + 1] * buf[slot, k]
    o_ref[...] = _layer_norm(y, g_ref[...], b_ref[...])


def _combine(dest3, gates, x1, g, b, ys, dn_alpha):
    t = x1.shape[0]
    n = t // TOK_BLOCK
    tok = pl.BlockSpec((TOK_BLOCK, D_MODEL), lambda i: (i, 0))
    vec = pl.BlockSpec((1, D_MODEL), lambda i: (0, 0))
    return pl.pallas_call(
        functools.partial(_combine_kernel, dn_alpha=dn_alpha), grid=(n,),
        in_specs=[pl.BlockSpec((1, 1, ROWS_PER_STEP), lambda i: (i, 0, 0), memory_space=pltpu.SMEM),
                  pl.BlockSpec((1, 1, ROWS_PER_STEP), lambda i: (jnp.minimum(i + 1, n - 1), 0, 0),
                               memory_space=pltpu.SMEM),
                  pl.BlockSpec((TOK_BLOCK, LANE), lambda i: (i, 0)), tok, vec, vec,
                  pl.BlockSpec(memory_space=pl.ANY)],
        out_specs=tok,
        out_shape=jax.ShapeDtypeStruct((t, D_MODEL), F32),
        scratch_shapes=[pltpu.VMEM((2, TOP_K, TOK_BLOCK, D_MODEL), F32), pltpu.SemaphoreType.DMA((2,))],
        compiler_params=_cparams("arbitrary"), name="moe_combine_ln",
    )(dest3, dest3, gates, x1, g, b, ys)


def _onehots(idx):
    lane = lax.broadcasted_iota(jnp.int32, idx.shape, 1)
    return lane, [lane == idx[:, k:k + 1] for k in range(TOP_K)]


def _lanes(cols, lane):
    out = jnp.zeros(lane.shape, jnp.int32)
    for k, c in enumerate(cols):
        out = jnp.where(lane == k, c, out)
    return out


def _rank_kernel(cnt0_ref, idx_ref, rank_ref, cnt_ref, carry_ref):
    @pl.when(pl.program_id(0) == 0)
    def _():
        carry_ref[...] = cnt0_ref[...]

    n = TOK_BLOCK
    r = lax.broadcasted_iota(jnp.int32, (n, n), 0)
    c = lax.broadcasted_iota(jnp.int32, (n, n), 1)
    earlier = jnp.where(c < r, 1.0, 0.0).astype(BF16)
    carry = carry_ref[...]
    for sub in range(idx_ref.shape[0] // n):
        rows = slice(sub * n, (sub + 1) * n)
        lane, hots = _onehots(idx_ref[rows, :])
        chose = sum(h.astype(F32) for h in hots)
        before = _dot(earlier, chose.astype(BF16)) + carry
        ranks = [jnp.sum(jnp.where(h, before, 0.0), axis=-1, keepdims=True).astype(jnp.int32) for h in hots]
        rank_ref[rows, :] = _lanes(ranks, lane)
        carry = carry + jnp.sum(chose, axis=0, keepdims=True)
    carry_ref[...] = carry
    cnt_ref[...] = carry


def _route_block(t):
    return 1024 if t % 1024 == 0 else TOK_BLOCK


def _rank(cnt0, idx):
    t = idx.shape[0]
    tb = _route_block(t)
    tok = pl.BlockSpec((tb, LANE), lambda i: (i, 0))
    one = pl.BlockSpec((1, LANE), lambda i: (0, 0))
    return pl.pallas_call(
        _rank_kernel, grid=(t // tb,), in_specs=[one, tok], out_specs=[tok, one],
        out_shape=[jax.ShapeDtypeStruct((t, LANE), jnp.int32), jax.ShapeDtypeStruct((1, LANE), F32)],
        scratch_shapes=[pltpu.VMEM((1, LANE), F32)],
        compiler_params=_cparams("arbitrary"), name="moe_rank",
    )(cnt0, idx)


def _dest_kernel(cnt_ref, idx_ref, rank_ref, dest_ref, blk_ref, used_ref, start_ref):
    @pl.when(pl.program_id(0) == 0)
    def _():
        lane = lax.broadcasted_iota(jnp.int32, (8, LANE), 1)
        cnt = jnp.broadcast_to(cnt_ref[...], (8, LANE)).astype(jnp.int32)
        nblk = jnp.where(lane < N_EXPERTS, jnp.right_shift(cnt + (MOE_TM - 1), MOE_TM.bit_length() - 1), 0)
        end = nblk
        for s in (1, 2, 4, 8, 16):
            end = end + jnp.where(lane >= s, pltpu.roll(end, s, 1), 0)
        start_ref[...] = ((end - nblk) * MOE_TM).astype(F32)
        sub = lax.broadcasted_iota(jnp.int32, (8, LANE), 0)
        cnt = jnp.where(lane < N_EXPERTS, cnt, 0)
        used_ref[...] = jnp.where(sub == 0, jnp.broadcast_to(end[:, N_EXPERTS - 1:N_EXPERTS], (8, LANE)),
                                  jnp.where(sub == 1, (end - nblk) * MOE_TM + cnt,
                                            jnp.where(sub == 2, nblk * MOE_TM - cnt, 0)))
        blk = lax.broadcasted_iota(jnp.int32, blk_ref.shape, 0)
        lane_b = lax.broadcasted_iota(jnp.int32, blk_ref.shape, 1)
        done = jnp.where(lane_b < N_EXPERTS, jnp.where(end[0:1, :] <= blk, 1.0, 0.0), 0.0)
        blk_ref[...] = jnp.broadcast_to(
            jnp.minimum(jnp.sum(done, axis=-1, keepdims=True), N_EXPERTS - 1.0).astype(jnp.int32), blk_ref.shape)

    lane, hots = _onehots(idx_ref[...])
    start = start_ref[0:1, :]
    rank = rank_ref[...]
    dest_ref[...] = _lanes([jnp.sum(jnp.where(h, start, 0.0), axis=-1, keepdims=True).astype(jnp.int32)
                            + rank[:, k:k + 1] for k, h in enumerate(hots)], lane)


def _dest(cnt, idx, rank, n_blocks):
    t = idx.shape[0]
    tb = _route_block(t)
    tok = pl.BlockSpec((tb, LANE), lambda i: (i, 0))
    rows = -(-n_blocks // 8) * 8
    return pl.pallas_call(
        _dest_kernel, grid=(t // tb,),
        in_specs=[pl.BlockSpec((1, LANE), lambda i: (0, 0)), tok, tok],
        out_specs=[tok, pl.BlockSpec((rows, LANE), lambda i: (0, 0)), pl.BlockSpec((8, LANE), lambda i: (0, 0))],
        out_shape=[jax.ShapeDtypeStruct((t, LANE), jnp.int32), jax.ShapeDtypeStruct((rows, LANE), jnp.int32),
                   jax.ShapeDtypeStruct((8, LANE), jnp.int32)],
        scratch_shapes=[pltpu.VMEM((8, LANE), F32)],
        compiler_params=_cparams("arbitrary"), name="moe_dest",
    )(cnt, idx, rank)


def _moe(groups, w_gu, b_gu, w_dn, b_dn, g, b, dn_alpha):
    n_assign = sum(x1.shape[0] for x1, _, _ in groups) * TOP_K
    n_blocks = n_assign // MOE_TM + N_EXPERTS
    cnt = jnp.zeros((1, LANE), F32)
    ranks = []
    for _, idx, _ in groups:
        rank, cnt = _rank(cnt, idx)
        ranks.append(rank)
    dests = []
    for (x1, idx, _), rank in zip(groups, ranks):
        dest, blk, used = _dest(cnt, idx, rank, n_blocks)
        dests.append(dest[:, :TOP_K].reshape(-1, 1, ROWS_PER_STEP))
    (xa, _, _), (xb, _, _) = groups
    xs = _scatter_rows(used[1, :N_EXPERTS], used[2, :N_EXPERTS], used[0, :1], jnp.concatenate(dests, axis=0),
                       xa, xb, n_blocks)
    ys = _experts(blk[:n_blocks, 0], used[0, :1], xs, w_gu, b_gu, w_dn, b_dn)
    return [_combine(dest3, gates, x1, g, b, ys, dn_alpha) for (x1, _, gates), dest3 in zip(groups, dests)]


def _project_all(x, w_in, pos, tm_plain, tm_rope, keys_t=None, operand_dtype=F32):
    n_groups = COL // LANE
    ret_rope = _rope_table(pos, R_DK) + (R_DK, (1.0,) * R_HEADS + (R_DK ** -0.5,) * R_HEADS)
    diff_tab = _rope_table(pos, D_DH)
    diff_rope = diff_tab + (D_DH, (1.0,) * n_groups)
    qk = _proj(x, w_in, 0, COL, tm_rope, ret_rope)
    vr = _proj(x, w_in, 1 * COL, COL, tm_plain, out_dtype=operand_dtype)
    gr = _proj(x, w_in, 2 * COL, COL, tm_plain)
    qd = _proj(x, w_in, 3 * COL, COL, tm_rope, diff_rope, out_dtype=operand_dtype)
    if keys_t is None:
        kd = _proj(x, w_in, 4 * COL, COL, tm_rope, diff_rope)
    else:
        cos_t, sin_t = (t[:, :D_DH].T for t in diff_tab)
        kd = _proj_keys_t(x, w_in, 4 * COL, tm_rope, cos_t, sin_t, *keys_t)
    vd = _proj(x, w_in, 5 * COL, COL, tm_plain)
    ga = _proj(x, w_in, 6 * COL, COL, tm_plain)
    gb = _proj(x, w_in, 7 * COL, COL, tm_plain)
    return qk, vr, gr, qd, kd, vd, ga, gb


def kernel(x_prompt, x_sample, cache_k, cache_v, state_ret, page_table, w_in, w_branch_ret, w_branch_diff, w_out, lam_q1, lam_k1, lam_q2, lam_k2, subln_w, ln1_g, ln1_b, w_router, b_router, w_gate_up, b_gate_up, w_down, b_down, ln2_g, ln2_b):
    depth = w_in.shape[0]
    assert depth == 1, "single-layer trunk"
    batch, seq, _ = x_prompt.shape
    dbatch, dseq, _ = x_sample.shape
    dn_alpha = (2.0 * depth) ** 0.25
    lam_init = 0.8 - 0.6 * math.exp(-0.3 * 0)
    n_p, n_s = batch * seq, dbatch * dseq

    w_in_b = w_in[0].astype(BF16)
    wr_b, wd_b, wo_b = w_branch_ret[0].astype(BF16), w_branch_diff[0].astype(BF16), w_out[0].astype(BF16)
    w_rt_b = w_router[0].astype(BF16)
    lam_args = (lam_q1, lam_k1, lam_q2, lam_k2, subln_w)

    xp = x_prompt.reshape(n_p, D_MODEL)
    qk, vr, gr, qd, kd_t, vd, ga, gb = _project_all(xp.astype(BF16), w_in_b, jnp.arange(seq), 1024, 512,
                                                    (batch, seq), BF16)
    o_r, s_p = _retention_prompt(qk, vr, gr, batch, seq)
    o_d = _dattn_prompt(lam_args, qd, kd_t, vd, batch, seq, lam_init)
    x1_p, idx_p, gate_p = _mix(xp, o_r, o_d, ga, gb, wr_b, wd_b, wo_b, ln1_g, ln1_b, w_rt_b, b_router, 512, dn_alpha)

    xs_ = x_sample.reshape(n_s, D_MODEL)
    pos_s = jnp.tile(PAST_LEN + jnp.arange(dseq), dbatch)
    qk_s, vr_s, gr_s, qd_s, kd_s, vd_s, ga_s, gb_s = _project_all(xs_, w_in_b, pos_s, n_s, n_s)
    pad8 = lambda a: jnp.pad(a.reshape(dbatch, dseq, -1), ((0, 0), (0, 8 - dseq), (0, 0)))
    pad_page = lambda a: jnp.pad(a.reshape(dbatch, dseq, -1), ((0, 0), (0, PAGE_SIZE - dseq), (0, 0)))
    o_r_s, s_s = _retention_sample(pad8(qk_s), pad8(vr_s), pad8(gr_s), state_ret[0], dseq)
    n_pool = cache_k.shape[1]
    cache_kt = jnp.transpose(cache_k[0], (0, 2, 3, 4, 1)).reshape(n_pool, D_HEADS * 2 * D_DH, PAGE_SIZE)
    cache_vr = cache_v[0].reshape(n_pool, PAGE_SIZE * D_HEADS, D_DV)
    kn_t = jnp.transpose(pad_page(kd_s), (0, 2, 1))
    vn = pad_page(vd_s).reshape(dbatch, PAGE_SIZE * D_HEADS, D_DV)
    o_d_s = _dattn_decode(lam_args, page_table, pad8(qd_s), kn_t, vn, cache_kt, cache_vr, lam_init)
    o_r_s = o_r_s[:, :dseq].reshape(n_s, -1)
    o_d_s = o_d_s[:, :dseq].reshape(n_s, -1)
    x1_s, idx_s, gate_s = _mix(xs_, o_r_s, o_d_s, ga_s, gb_s, wr_b, wd_b, wo_b, ln1_g, ln1_b, w_rt_b, b_router,
                               n_s, dn_alpha)

    y_p, y_s = _moe([(x1_p, idx_p, gate_p), (x1_s, idx_s, gate_s)],
                    w_gate_up[0], b_gate_up[0][:, None, :], w_down[0], b_down[0][:, None, :], ln2_g, ln2_b, dn_alpha)

    return (y_p.reshape(batch, seq, D_MODEL),
            y_s.reshape(dbatch, dseq, D_MODEL),
            jnp.transpose(kd_t.reshape(batch, D_HEADS, 2, D_DH, seq), (0, 4, 1, 2, 3))[None],
            vd.reshape(1, batch, seq, D_HEADS, D_DV),
            s_p[None],
            kd_s.reshape(1, dbatch, dseq, D_HEADS, 2, D_DH),
            vd_s.reshape(1, dbatch, dseq, D_HEADS, D_DV),
            s_s[None])
```

```python
import functools
import math

import jax
import jax.numpy as jnp
from jax import lax
from jax.experimental import pallas as pl
from jax.experimental.pallas import tpu as pltpu

F32 = jnp.float32
BF16 = jnp.bfloat16

D_MODEL = 1024
PAST_LEN = 16384
PAGE_SIZE = 128
R_HEADS, R_DK, R_DV, R_CHUNK = 4, 128, 256, 128
D_HEADS, D_DH, D_DV = 8, 64, 128
ROPE_THETA = 10000.0
N_EXPERTS, TOP_K, D_FF = 32, 4, 1024
SWIGLU_ALPHA, SWIGLU_LIMIT = 1.702, 7.0
NEG_INF = -1e30

LANE = 128
VMEM_LIMIT = 56 * 1024 * 1024
ATTN_BLOCK = 256
DEC_PAGES = 16
MOE_TM = 512
TOK_BLOCK = 128
COL = 1024


def _cparams(*sem):
    return pltpu.CompilerParams(dimension_semantics=sem, vmem_limit_bytes=VMEM_LIMIT)


def _dot(a, b):
    return jnp.dot(a, b, preferred_element_type=F32)


def _dot_nt(a, b):
    return lax.dot_general(a, b, (((1,), (1,)), ((), ())), preferred_element_type=F32)


def _swap_halves(sl, unit):
    if unit == LANE:
        return pltpu.roll(sl, LANE // 2, 1)
    lane = lax.broadcasted_iota(jnp.int32, sl.shape, 1)
    fwd = pltpu.roll(sl, unit // 2, 1)
    bwd = pltpu.roll(sl, LANE - unit // 2, 1)
    return jnp.where((lane % unit) < unit // 2, bwd, fwd)


def _proj_kernel(x_ref, w_ref, o_ref):
    o_ref[...] = _dot(x_ref[...].astype(BF16), w_ref[...]).astype(o_ref.dtype)


def _proj_rope_kernel(x_ref, w_ref, cos_ref, sin_ref, o_ref, *, unit, scales):
    h = _dot(x_ref[...].astype(BF16), w_ref[...])
    cos, sin = cos_ref[...], sin_ref[...]
    for g, scale in enumerate(scales):
        sl = h[:, g * LANE:(g + 1) * LANE]
        o = sl * cos + _swap_halves(sl, unit) * sin
        if scale != 1.0:
            o = o * scale
        o_ref[:, g * LANE:(g + 1) * LANE] = o.astype(o_ref.dtype)


def _proj_rope_t_kernel(x_ref, w_ref, cos_ref, sin_ref, o_ref):
    ht = _dot(x_ref[...].astype(BF16), w_ref[...]).T
    half = D_DH // 2
    cos, sin = cos_ref[...], sin_ref[...]
    parts = []
    for u in range(ht.shape[0] // D_DH):
        blk = ht[u * D_DH:(u + 1) * D_DH]
        swapped = jnp.concatenate([blk[half:], blk[:half]], axis=0)
        parts.append(blk * cos + swapped * sin)
    o_ref[...] = jnp.concatenate(parts, axis=0)


def _proj_keys_t(x, w, col0, tm, cos_t, sin_t, batch, seq):
    n_pos = seq // tm
    return pl.pallas_call(
        _proj_rope_t_kernel, grid=(batch * n_pos,),
        in_specs=[pl.BlockSpec((tm, D_MODEL), lambda i: (i, 0)),
                  pl.BlockSpec((D_MODEL, COL), lambda i: (0, col0 // COL)),
                  pl.BlockSpec((D_DH, tm), lambda i: (0, i % n_pos)),
                  pl.BlockSpec((D_DH, tm), lambda i: (0, i % n_pos))],
        out_specs=pl.BlockSpec((COL, tm), lambda i: (i // n_pos, i % n_pos)),
        out_shape=jax.ShapeDtypeStruct((batch * COL, seq), F32),
        compiler_params=_cparams("parallel"), name="proj_keys_t",
    )(x, w, cos_t, sin_t)


def _proj(x, w, col0, ncols, tm, rope=None, out_dtype=F32):
    t = x.shape[0]
    grid = (t // tm, ncols // COL)
    in_specs = [pl.BlockSpec((tm, D_MODEL), lambda i, j: (i, 0)),
                pl.BlockSpec((D_MODEL, COL), lambda i, j: (0, col0 // COL + j))]
    args = [x, w]
    if rope is None:
        body = _proj_kernel
    else:
        cos, sin, unit, scales = rope
        n_pos = cos.shape[0] // tm
        in_specs += [pl.BlockSpec((tm, LANE), lambda i, j: (i % n_pos, 0))] * 2
        args += [cos, sin]
        body = functools.partial(_proj_rope_kernel, unit=unit, scales=scales)
    return pl.pallas_call(
        body, grid=grid, in_specs=in_specs,
        out_specs=pl.BlockSpec((tm, COL), lambda i, j: (i, j)),
        out_shape=jax.ShapeDtypeStruct((t, ncols), out_dtype),
        compiler_params=_cparams("parallel", "arbitrary"), name="proj" if rope is None else "proj_rope",
    )(*args)


def _rope_table(pos, unit):
    half = unit // 2
    inv = ROPE_THETA ** (-jnp.arange(half, dtype=F32) / half)
    ang = pos.astype(F32)[:, None] * inv[None, :]
    cos, sin = jnp.cos(ang), jnp.sin(ang)
    c = jnp.concatenate([cos, cos], axis=-1)
    s = jnp.concatenate([-sin, sin], axis=-1)
    return jnp.tile(c, (1, LANE // unit)), jnp.tile(s, (1, LANE // unit))


def _ret_tables(chunk, n_tok):
    log_gamma = jnp.log(1.0 - 2.0 ** (-5.0 - jnp.arange(R_HEADS, dtype=F32)))
    idx = jnp.arange(chunk, dtype=F32)
    diff = idx[:, None] - idx[None, :]
    causal = diff >= 0
    dec = jnp.where(causal[None], jnp.exp(log_gamma[:, None, None] * jnp.where(causal, diff, 0.0)[None]), 0.0)
    qdec = jnp.exp(log_gamma[:, None] * (idx[None, :] + 1.0))
    kdec = jnp.exp(log_gamma[:, None] * (n_tok - 1.0 - idx[None, :]))
    sdec = jnp.exp(log_gamma * n_tok)
    return (dec,
            jnp.broadcast_to(qdec[:, :, None], (R_HEADS, chunk, R_DV)),
            jnp.broadcast_to(kdec[:, :, None], (R_HEADS, chunk, R_DK)),
            jnp.broadcast_to(sdec[:, None, None], (R_HEADS, 1, R_DV)))


def _ret_chunk(q, k, v, g, state, dec, qdec, kdec, sdec):
    qb, kb, vb = q.astype(BF16), k.astype(BF16), v.astype(BF16)
    scores = _dot_nt(qb, kb) * dec
    inner = _dot(scores.astype(BF16), vb)
    cross = _dot(qb, state.astype(BF16)) * qdec
    o = inner + cross
    kd_t = (k * kdec).T.astype(BF16)
    s_new = state * sdec + _dot(kd_t, vb)
    mu = jnp.mean(o, axis=-1, keepdims=True)
    var = jnp.mean(jnp.square(o - mu), axis=-1, keepdims=True)
    o = (o - mu) * lax.rsqrt(var + 1e-6)
    return o * (g * jax.nn.sigmoid(g)), s_new


def _ret_prompt_kernel(q_ref, k_ref, v_ref, g_ref, dec_ref, qdec_ref, kdec_ref, sdec_ref, o_ref, s_ref):
    @pl.when(pl.program_id(1) == 0)
    def _():
        s_ref[...] = jnp.zeros_like(s_ref)

    for b in range(RET_SEQS):
        for h in range(R_HEADS):
            dk, dv = slice(h * R_DK, (h + 1) * R_DK), slice(h * R_DV, (h + 1) * R_DV)
            o, s_new = _ret_chunk(q_ref[b, :, dk], k_ref[b, :, dk], v_ref[b, :, dv], g_ref[b, :, dv], s_ref[b, h],
                                  dec_ref[h], qdec_ref[h], kdec_ref[h], sdec_ref[h])
            o_ref[b, :, dv] = o.astype(o_ref.dtype)
            s_ref[b, h] = s_new


RET_SEQS = 2


def _retention_prompt(qk, vr, gr, batch, seq):
    nc = seq // R_CHUNK
    tabs = _ret_tables(R_CHUNK, R_CHUNK)
    whole = lambda a: pl.BlockSpec(a.shape, lambda b, c: (0,) * a.ndim)
    seq3 = lambda a: a.reshape(batch, seq, a.shape[-1])
    o, s = pl.pallas_call(
        _ret_prompt_kernel, grid=(batch // RET_SEQS, nc),
        in_specs=[
            pl.BlockSpec((RET_SEQS, R_CHUNK, R_HEADS * R_DK), lambda b, c: (b, c, 0)),
            pl.BlockSpec((RET_SEQS, R_CHUNK, R_HEADS * R_DK), lambda b, c: (b, c, 1)),
            pl.BlockSpec((RET_SEQS, R_CHUNK, R_HEADS * R_DV), lambda b, c: (b, c, 0)),
            pl.BlockSpec((RET_SEQS, R_CHUNK, R_HEADS * R_DV), lambda b, c: (b, c, 0)),
        ] + [whole(a) for a in tabs],
        out_specs=[
            pl.BlockSpec((RET_SEQS, R_CHUNK, R_HEADS * R_DV), lambda b, c: (b, c, 0)),
            pl.BlockSpec((RET_SEQS, R_HEADS, R_DK, R_DV), lambda b, c: (b, 0, 0, 0)),
        ],
        out_shape=[jax.ShapeDtypeStruct((batch, seq, R_HEADS * R_DV), BF16),
                   jax.ShapeDtypeStruct((batch, R_HEADS, R_DK, R_DV), F32)],
        compiler_params=_cparams("parallel", "arbitrary"), name="retention_prompt",
    )(seq3(qk), seq3(qk), seq3(vr), seq3(gr), *tabs)
    return o.reshape(batch * seq, -1), s


def _ret_sample_kernel(q_ref, k_ref, v_ref, g_ref, s0_ref, dec_ref, qdec_ref, kdec_ref, sdec_ref,
                       o_ref, s_ref, qp, kp, vp, gp):
    n = q_ref.shape[1]
    for pad, src in ((qp, q_ref), (kp, k_ref), (vp, v_ref), (gp, g_ref)):
        pad[...] = jnp.zeros_like(pad)
        pad[0:n, :] = src[0]
    o, s_new = _ret_chunk(qp[...], kp[...], vp[...], gp[...], s0_ref[0, 0],
                          dec_ref[0], qdec_ref[0], kdec_ref[0], sdec_ref[0])
    o_ref[0] = o[0:n, :]
    s_ref[0, 0] = s_new


def _retention_sample(qk, vr, gr, state, n_tok):
    batch, rows, _ = qk.shape
    tabs = _ret_tables(R_CHUNK, n_tok)
    return pl.pallas_call(
        _ret_sample_kernel, grid=(batch, R_HEADS),
        in_specs=[
            pl.BlockSpec((1, rows, R_DK), lambda b, h: (b, 0, h)),
            pl.BlockSpec((1, rows, R_DK), lambda b, h: (b, 0, R_HEADS + h)),
            pl.BlockSpec((1, rows, R_DV), lambda b, h: (b, 0, h)),
            pl.BlockSpec((1, rows, R_DV), lambda b, h: (b, 0, h)),
            pl.BlockSpec((1, 1, R_DK, R_DV), lambda b, h: (b, h, 0, 0)),
            pl.BlockSpec((1, R_CHUNK, R_CHUNK), lambda b, h: (h, 0, 0)),
            pl.BlockSpec((1, R_CHUNK, R_DV), lambda b, h: (h, 0, 0)),
            pl.BlockSpec((1, R_CHUNK, R_DK), lambda b, h: (h, 0, 0)),
            pl.BlockSpec((1, 1, R_DV), lambda b, h: (h, 0, 0)),
        ],
        out_specs=[
            pl.BlockSpec((1, rows, R_DV), lambda b, h: (b, 0, h)),
            pl.BlockSpec((1, 1, R_DK, R_DV), lambda b, h: (b, h, 0, 0)),
        ],
        out_shape=[jax.ShapeDtypeStruct((batch, rows, R_HEADS * R_DV), F32),
                   jax.ShapeDtypeStruct((batch, R_HEADS, R_DK, R_DV), F32)],
        scratch_shapes=[pltpu.VMEM((R_CHUNK, R_DK), F32), pltpu.VMEM((R_CHUNK, R_DK), F32),
                        pltpu.VMEM((R_CHUNK, R_DV), F32), pltpu.VMEM((R_CHUNK, R_DV), F32)],
        compiler_params=_cparams("parallel", "parallel"), name="retention_sample",
    )(qk, qk, vr, gr, state, *tabs)


def _lambda(lq1_ref, lk1_ref, lq2_ref, lk2_ref, lam_init):
    a = jnp.sum(lq1_ref[...] * lk1_ref[...], axis=-1, keepdims=True)
    b = jnp.sum(lq2_ref[...] * lk2_ref[...], axis=-1, keepdims=True)
    return jnp.exp(a) - jnp.exp(b) + lam_init


def _sub_rms(o, w, lam_init):
    ms = jnp.mean(jnp.square(o), axis=-1, keepdims=True)
    return o * lax.rsqrt(ms + 1e-5) * w * (1.0 - lam_init)


def _dattn_prompt_kernel(lq1_ref, lk1_ref, lq2_ref, lk2_ref, w_ref, q_ref, kt_ref, v_ref, o_ref,
                         kc_ref, vx_ref, *, blk, lam_init):
    kc_ref[...] = kt_ref[...].astype(BF16)
    vx_ref[:, 0:D_DV] = v_ref[...].astype(BF16)
    vx_ref[:, D_DV:2 * D_DV] = jnp.ones((vx_ref.shape[0], D_DV), BF16)
    row = lax.broadcasted_iota(jnp.int32, (blk, blk), 0)
    col = lax.broadcasted_iota(jnp.int32, (blk, blk), 1)
    lam = _lambda(lq1_ref, lk1_ref, lq2_ref, lk2_ref, lam_init)

    for qi in range(q_ref.shape[0] // blk):
        q = q_ref[qi * blk:(qi + 1) * blk, :] * (D_DH ** -0.5)
        outs = []
        for c in range(2):
            qc = q[:, c * D_DH:(c + 1) * D_DH].astype(BF16)
            m = jnp.full((blk, 1), NEG_INF, F32)
            acc = jnp.zeros((blk, 2 * D_DV), F32)
            for j in range(qi + 1):
                s = _dot(qc, kc_ref[c * D_DH:(c + 1) * D_DH, j * blk:(j + 1) * blk])
                if j == qi:
                    s = jnp.where(col <= row, s, NEG_INF)
                m_new = jnp.maximum(m, jnp.max(s, axis=-1, keepdims=True))
                p = jnp.exp(s - m_new)
                acc = acc * jnp.exp(m - m_new) + _dot(p.astype(BF16), vx_ref[j * blk:(j + 1) * blk, :])
                m = m_new
            outs.append(acc[:, 0:D_DV] / acc[:, D_DV:D_DV + 1])
        o_ref[qi * blk:(qi + 1) * blk, :] = _sub_rms(outs[0] - lam * outs[1], w_ref[...],
                                                     lam_init).astype(o_ref.dtype)


def _lam_specs(n):
    zero = lambda *_: (0, 0)
    return [pl.BlockSpec((1, D_DH), zero)] * 4 + [pl.BlockSpec((1, D_DV), zero)]


def _dattn_prompt(lam_args, qd, kd_t, vd, batch, seq, lam_init):
    blk = min(ATTN_BLOCK, seq)
    return pl.pallas_call(
        functools.partial(_dattn_prompt_kernel, blk=blk, lam_init=lam_init),
        grid=(batch, D_HEADS),
        in_specs=_lam_specs(2) + [
            pl.BlockSpec((seq, 2 * D_DH), lambda b, h: (b, h)),
            pl.BlockSpec((2 * D_DH, seq), lambda b, h: (b * D_HEADS + h, 0)),
            pl.BlockSpec((seq, D_DV), lambda b, h: (b, h)),
        ],
        out_specs=pl.BlockSpec((seq, D_DV), lambda b, h: (b, h)),
        out_shape=jax.ShapeDtypeStruct((batch * seq, D_HEADS * D_DV), BF16),
        scratch_shapes=[pltpu.VMEM((2 * D_DH, seq), BF16), pltpu.VMEM((seq, 2 * D_DV), BF16)],
        compiler_params=_cparams("parallel", "parallel"), name="dattn_prompt",
    )(*lam_args, qd, kd_t, vd)


DEC_ROWS = D_HEADS * 2 * 8


def _dattn_decode_kernel(pt_ref, lq1_ref, lk1_ref, lq2_ref, lk2_ref, w_ref, q_ref, kn_ref, vn_ref, ck_ref, cv_ref,
                         o_ref, qbd_ref, m_ref, l_ref, acc_ref, kbuf, vbuf, sems, *, n_pages, lam_init):
    j = pl.program_id(1)
    step = pl.program_id(0) * pl.num_programs(1) + j
    slot = step % 2

    def fetch(s, into):
        for p in range(n_pages):
            page = pt_ref[s * n_pages + p]
            pltpu.make_async_copy(ck_ref.at[page], kbuf.at[into, p], sems.at[0, into]).start()
            pltpu.make_async_copy(cv_ref.at[page], vbuf.at[into, p], sems.at[1, into]).start()

    @pl.when(step == 0)
    def _():
        fetch(0, 0)

    @pl.when(step + 1 < pl.num_programs(0) * pl.num_programs(1))
    def _():
        fetch(step + 1, 1 - slot)

    pltpu.make_async_copy(ck_ref.at[pl.ds(0, n_pages)], kbuf.at[slot], sems.at[0, slot]).wait()
    pltpu.make_async_copy(cv_ref.at[pl.ds(0, n_pages)], vbuf.at[slot], sems.at[1, slot]).wait()

    @pl.when(j == 0)
    def _():
        q8 = q_ref[0] * (D_DH ** -0.5)
        tiled = jnp.concatenate([q8] * (DEC_ROWS // 8), axis=0)
        r = lax.broadcasted_iota(jnp.int32, tiled.shape, 0)
        cidx = lax.broadcasted_iota(jnp.int32, tiled.shape, 1)
        qbd_ref[...] = jnp.where(r // 8 == cidx // D_DH, tiled, 0.0).astype(BF16)
        m_ref[...] = jnp.full(m_ref.shape, NEG_INF, F32)
        l_ref[...] = jnp.zeros(l_ref.shape, F32)
        acc_ref[...] = jnp.zeros(acc_ref.shape, F32)

    rows_h = DEC_ROWS // D_HEADS

    def update(s_list, v_pages):
        m = m_ref[...]
        m_new = m
        for s in s_list:
            m_new = jnp.maximum(m_new, jnp.max(s, axis=-1, keepdims=True))
        alpha = jnp.exp(m - m_new)
        p = jnp.concatenate([jnp.exp(s - m_new) for s in s_list], axis=1)
        l_ref[...] = l_ref[...] * alpha + jnp.sum(p, axis=-1, keepdims=True)
        m_ref[...] = m_new
        p = p.astype(BF16)
        for h in range(D_HEADS):
            rows = slice(h * rows_h, (h + 1) * rows_h)
            v_h = jnp.concatenate([v[pl.ds(h, PAGE_SIZE, stride=D_HEADS), :] for v in v_pages], axis=0)
            acc_ref[rows, :] = acc_ref[rows, :] * alpha[rows, :] + _dot(p[rows, :], v_h.astype(BF16))

    qbd = qbd_ref[...]
    update([_dot(qbd, kbuf[slot, p].astype(BF16)) for p in range(n_pages)],
           [vbuf.at[slot, p] for p in range(n_pages)])

    @pl.when(j == pl.num_programs(1) - 1)
    def _():
        s = _dot(qbd, kn_ref[0].astype(BF16))
        t_q = lax.broadcasted_iota(jnp.int32, s.shape, 0) % 8
        t_k = lax.broadcasted_iota(jnp.int32, s.shape, 1)
        update([jnp.where(t_k <= t_q, s, NEG_INF)], [vn_ref.at[0]])
        acc = acc_ref[...] / l_ref[...]
        lam = _lambda(lq1_ref, lk1_ref, lq2_ref, lk2_ref, lam_init)
        w = w_ref[...]
        heads = []
        for h in range(D_HEADS):
            o = acc[h * rows_h:h * rows_h + 8, :] - lam * acc[h * rows_h + 8:(h + 1) * rows_h, :]
            heads.append(_sub_rms(o, w, lam_init))
        o_ref[0] = jnp.concatenate(heads, axis=1)


def _dattn_decode(lam_args, page_table, q8, kn_t, vn, cache_kt, cache_v, lam_init):
    batch, n_pages = page_table.shape
    steps = n_pages // DEC_PAGES
    width = q8.shape[-1]
    page = (1,) + cache_kt.shape[1:]
    pages = (2, DEC_PAGES) + cache_kt.shape[1:]
    zero = lambda b, j, pt: (0, 0)
    seq_spec = lambda shape: pl.BlockSpec(shape, lambda b, j, pt: (b, 0, 0))
    grid_spec = pltpu.PrefetchScalarGridSpec(
        num_scalar_prefetch=1, grid=(batch, steps),
        in_specs=([pl.BlockSpec((1, D_DH), zero)] * 4 + [pl.BlockSpec((1, D_DV), zero)]
                  + [seq_spec((1, 8, width)), seq_spec(page), seq_spec(page)]
                  + [pl.BlockSpec(memory_space=pl.ANY)] * 2),
        out_specs=seq_spec((1, 8, width)),
        scratch_shapes=[pltpu.VMEM((DEC_ROWS, width), BF16), pltpu.VMEM((DEC_ROWS, 1), F32),
                        pltpu.VMEM((DEC_ROWS, 1), F32), pltpu.VMEM((DEC_ROWS, D_DV), F32),
                        pltpu.VMEM(pages, F32), pltpu.VMEM(pages, F32), pltpu.SemaphoreType.DMA((2, 2))])
    return pl.pallas_call(
        functools.partial(_dattn_decode_kernel, n_pages=DEC_PAGES, lam_init=lam_init),
        grid_spec=grid_spec,
        out_shape=jax.ShapeDtypeStruct((batch, 8, width), F32),
        compiler_params=_cparams("arbitrary", "arbitrary"), name="dattn_decode",
    )(page_table.reshape(-1), *lam_args, q8, kn_t, vn, cache_kt, cache_v)


def _layer_norm(y, g, b):
    mu = jnp.mean(y, axis=-1, keepdims=True)
    var = jnp.mean(jnp.square(y - mu), axis=-1, keepdims=True)
    return (y - mu) * lax.rsqrt(var + 1e-5) * g + b


def _mix_kernel(x_ref, or_ref, od_ref, ga_ref, gb_ref, wr_ref, wd_ref, wo_ref, g_ref, b_ref, wrt_ref, brt_ref,
                x1_ref, idx_ref, gate_ref, *, dn_alpha):
    r = _dot(or_ref[...].astype(BF16), wr_ref[...])
    d = _dot(od_ref[...].astype(BF16), wd_ref[...])
    mix = jax.nn.sigmoid(ga_ref[...]) * r + jax.nn.sigmoid(gb_ref[...]) * d
    y = dn_alpha * x_ref[...] + _dot(mix.astype(BF16), wo_ref[...])
    x1 = _layer_norm(y, g_ref[...], b_ref[...])
    x1_ref[...] = x1
    vals = _dot(x1.astype(BF16), wrt_ref[...]) + brt_ref[...]
    col = lax.broadcasted_iota(jnp.int32, vals.shape, 1)
    lane = lax.broadcasted_iota(jnp.int32, idx_ref.shape, 1)
    idx_out = jnp.zeros(idx_ref.shape, jnp.int32)
    val_out = jnp.full(gate_ref.shape, NEG_INF, F32)
    for k in range(TOP_K):
        mx = jnp.max(vals, axis=-1, keepdims=True)
        first = jnp.min(jnp.where(vals == mx, col, N_EXPERTS), axis=-1, keepdims=True)
        idx_out = jnp.where(lane == k, first, idx_out)
        val_out = jnp.where(lane == k, mx, val_out)
        vals = jnp.where(col == first, NEG_INF, vals)
    e = jnp.exp(val_out - jnp.max(val_out, axis=-1, keepdims=True))
    idx_ref[...] = idx_out
    gate_ref[...] = e / jnp.sum(e, axis=-1, keepdims=True)


def _mix(x, o_r, o_d, ga, gb, wr, wd, wo, g, b, w_rt, b_rt, tm, dn_alpha):
    t = x.shape[0]
    tok = pl.BlockSpec((tm, D_MODEL), lambda i: (i, 0))
    mat = pl.BlockSpec((D_MODEL, D_MODEL), lambda i: (0, 0))
    vec = pl.BlockSpec((1, D_MODEL), lambda i: (0, 0))
    return pl.pallas_call(
        functools.partial(_mix_kernel, dn_alpha=dn_alpha), grid=(t // tm,),
        in_specs=[tok] * 5 + [mat] * 3 + [vec, vec,
                                          pl.BlockSpec((D_MODEL, N_EXPERTS), lambda i: (0, 0)),
                                          pl.BlockSpec((1, N_EXPERTS), lambda i: (0, 0))],
        out_specs=[tok, pl.BlockSpec((tm, LANE), lambda i: (i, 0)), pl.BlockSpec((tm, LANE), lambda i: (i, 0))],
        out_shape=[jax.ShapeDtypeStruct((t, D_MODEL), F32), jax.ShapeDtypeStruct((t, LANE), jnp.int32),
                   jax.ShapeDtypeStruct((t, LANE), F32)],
        compiler_params=_cparams("parallel"), name="mix_ln_router",
    )(x, o_r, o_d, ga, gb, wr, wd, wo, g, b, w_rt, b_rt)


ROWS_PER_STEP = TOK_BLOCK * TOP_K


SCATTER_SLOTS = 3


def _zero_padding(pad_begin_ref, pad_len_ref, n_used_ref, xs_ref, zbuf, sem, first_tail, n_blocks):
    zbuf[...] = jnp.zeros_like(zbuf)
    bits = MOE_TM.bit_length() - 1

    def pieces():
        for e in range(N_EXPERTS):
            begin, length = pad_begin_ref[e], pad_len_ref[e]
            head = jnp.bitwise_and(-begin, 7)
            for k in range(7):
                yield k < head, begin + k, 1
            begin8, length8 = begin + head, length - head
            for bit in range(3, bits):
                higher = jnp.left_shift(jnp.right_shift(length8, bit + 1), bit + 1)
                yield (jnp.bitwise_and(jnp.right_shift(length8, bit), 1) == 1,
                       pl.multiple_of(begin8 + higher, 8), 1 << bit)
        for blk in range(first_tail, n_blocks):
            yield blk >= n_used_ref[0], blk * MOE_TM, MOE_TM

    for wait in (False, True):
        for cond, row, rows in pieces():
            @pl.when(cond)
            def _(row=row, rows=rows):
                copy = pltpu.make_async_copy(zbuf.at[pl.ds(0, rows)], xs_ref.at[pl.ds(row, rows)], sem)
                copy.wait() if wait else copy.start()


def _scatter_rows_kernel(pad_begin_ref, pad_len_ref, n_used_ref, dest_ref, xa_ref, xb_ref, xs_ref,
                         xbuf, zbuf, load_sems, row_sems, zero_sem, *, n_first, first_tail, n_blocks):
    i = pl.program_id(0)
    n = pl.num_programs(0)
    slot = i % SCATTER_SLOTS

    def load_start(step):
        s = step % SCATTER_SLOTS

        @pl.when(step < n_first)
        def _():
            rows = pl.ds(pl.multiple_of(step * TOK_BLOCK, TOK_BLOCK), TOK_BLOCK)
            pltpu.make_async_copy(xa_ref.at[rows], xbuf.at[s], load_sems.at[s]).start()

        @pl.when(step >= n_first)
        def _():
            rows = pl.ds(pl.multiple_of((step - n_first) * TOK_BLOCK, TOK_BLOCK), TOK_BLOCK)
            pltpu.make_async_copy(xb_ref.at[rows], xbuf.at[s], load_sems.at[s]).start()

    @pl.when(i == 0)
    def _():
        load_start(0)

        @pl.when(n > 1)
        def _():
            load_start(1)

        _zero_padding(pad_begin_ref, pad_len_ref, n_used_ref, xs_ref, zbuf, zero_sem, first_tail, n_blocks)

    pltpu.make_async_copy(xa_ref.at[pl.ds(0, TOK_BLOCK)], xbuf.at[slot], load_sems.at[slot]).wait()

    def start(t, c):
        for k in range(TOP_K):
            pltpu.make_async_copy(xbuf.at[slot, pl.ds(t, 1)], xs_ref.at[pl.ds(dest_ref[0, 0, t * TOP_K + k], 1)],
                                  row_sems.at[slot]).start(priority=k % 2)
        return c

    lax.fori_loop(0, TOK_BLOCK, start, 0, unroll=8)

    def drain(s):
        pltpu.make_async_copy(xs_ref.at[pl.ds(0, ROWS_PER_STEP)], xs_ref.at[pl.ds(ROWS_PER_STEP, ROWS_PER_STEP)],
                              row_sems.at[s]).wait()

    @pl.when(i > 0)
    def _():
        drain((i - 1) % SCATTER_SLOTS)

    @pl.when(i + 2 < n)
    def _():
        load_start(i + 2)

    @pl.when(i == n - 1)
    def _():
        drain(slot)


def _scatter_rows(pad_begin, pad_len, n_used, dest3, xa, xb, n_blocks):
    n_first = xa.shape[0] // TOK_BLOCK
    n_steps = n_first + xb.shape[0] // TOK_BLOCK
    first_tail = n_steps * ROWS_PER_STEP // MOE_TM
    grid_spec = pltpu.PrefetchScalarGridSpec(
        num_scalar_prefetch=3, grid=(n_steps,),
        in_specs=[pl.BlockSpec((1, 1, ROWS_PER_STEP), lambda i, *_: (i, 0, 0), memory_space=pltpu.SMEM),
                  pl.BlockSpec(memory_space=pl.ANY),
                  pl.BlockSpec(memory_space=pl.ANY)],
        out_specs=pl.BlockSpec(memory_space=pl.ANY),
        scratch_shapes=[pltpu.VMEM((SCATTER_SLOTS, TOK_BLOCK, D_MODEL), F32), pltpu.VMEM((MOE_TM, D_MODEL), F32),
                        pltpu.SemaphoreType.DMA((SCATTER_SLOTS,)), pltpu.SemaphoreType.DMA((SCATTER_SLOTS,)),
                        pltpu.SemaphoreType.DMA])
    return pl.pallas_call(
        functools.partial(_scatter_rows_kernel, n_first=n_first, first_tail=first_tail, n_blocks=n_blocks),
        grid_spec=grid_spec,
        out_shape=jax.ShapeDtypeStruct((n_blocks * MOE_TM, D_MODEL), F32),
        compiler_params=_cparams("arbitrary"), name="moe_scatter_rows",
    )(pad_begin, pad_len, n_used, dest3, xa, xb)


def _expert_kernel(blk_e_ref, n_used_ref, x_ref, wgu_ref, bgu_ref, wdn_ref, bdn_ref, y_ref, wgu_b, wdn_b):
    i = pl.program_id(0)
    used = i < n_used_ref[0]

    @pl.when(jnp.logical_not(used))
    def _():
        y_ref[...] = jnp.zeros_like(y_ref)

    first = jnp.logical_or(i == 0, blk_e_ref[i] != blk_e_ref[jnp.maximum(i - 1, 0)])

    @pl.when(jnp.logical_and(used, first))
    def _():
        wgu_b[...] = wgu_ref[0].astype(BF16)
        wdn_b[...] = wdn_ref[0].astype(BF16)

    @pl.when(used)
    def _():
        hgu = _dot(x_ref[...].astype(BF16), wgu_b[...]) + bgu_ref[0]
        glu = jnp.minimum(hgu[:, 0:D_FF], SWIGLU_LIMIT)
        lin = jnp.clip(hgu[:, D_FF:2 * D_FF], -SWIGLU_LIMIT, SWIGLU_LIMIT)
        act = glu * jax.nn.sigmoid(SWIGLU_ALPHA * glu) * (lin + 1.0)
        y_ref[...] = _dot(act.astype(BF16), wdn_b[...]) + bdn_ref[0]


def _experts(blk_e, n_used, xs, w_gu, b_gu, w_dn, b_dn):
    n_rows = xs.shape[0]
    blk = lambda i, be, nu: jnp.minimum(i, nu[0] - 1)
    exp = lambda i, be, nu: be[jnp.minimum(i, nu[0] - 1)]
    grid_spec = pltpu.PrefetchScalarGridSpec(
        num_scalar_prefetch=2, grid=(n_rows // MOE_TM,),
        in_specs=[pl.BlockSpec((MOE_TM, D_MODEL), lambda i, be, nu: (blk(i, be, nu), 0)),
                  pl.BlockSpec((1, D_MODEL, 2 * D_FF), lambda i, be, nu: (exp(i, be, nu), 0, 0)),
                  pl.BlockSpec((1, 1, 2 * D_FF), lambda i, be, nu: (exp(i, be, nu), 0, 0)),
                  pl.BlockSpec((1, D_FF, D_MODEL), lambda i, be, nu: (exp(i, be, nu), 0, 0)),
                  pl.BlockSpec((1, 1, D_MODEL), lambda i, be, nu: (exp(i, be, nu), 0, 0))],
        out_specs=pl.BlockSpec((MOE_TM, D_MODEL), lambda i, be, nu: (i, 0)),
        scratch_shapes=[pltpu.VMEM((D_MODEL, 2 * D_FF), BF16), pltpu.VMEM((D_FF, D_MODEL), BF16)])
    return pl.pallas_call(
        _expert_kernel, grid_spec=grid_spec,
        out_shape=jax.ShapeDtypeStruct((n_rows, D_MODEL), F32),
        compiler_params=_cparams("arbitrary"), name="moe_experts",
    )(blk_e, n_used, xs, w_gu, b_gu, w_dn, b_dn)


def _combine_kernel(dest_ref, next_ref, gate_ref, x_ref, g_ref, b_ref, ys_ref, o_ref, buf, sems, *, dn_alpha):
    i = pl.program_id(0)
    slot = i % 2

    def issue(d_ref, s):
        def start(t, c):
            for k in range(TOP_K):
                pltpu.make_async_copy(ys_ref.at[pl.ds(d_ref[0, 0, t * TOP_K + k], 1)],
                                      buf.at[s, k, pl.ds(t, 1)], sems.at[s]).start(priority=k % 2)
            return c

        lax.fori_loop(0, TOK_BLOCK, start, 0, unroll=8)

    @pl.when(i == 0)
    def _():
        issue(dest_ref, 0)

    @pl.when(i + 1 < pl.num_programs(0))
    def _():
        issue(next_ref, 1 - slot)

    for k in range(TOP_K):
        pltpu.make_async_copy(ys_ref.at[pl.ds(0, TOK_BLOCK)], buf.at[slot, k], sems.at[slot]).wait()
    gate = gate_ref[...]
    y = dn_alpha * x_ref[...]
    for k in range(TOP_K):
        y = y + gate[:, k:k + 1] * buf[slot, k]
    o_ref[...] = _layer_norm(y, g_ref[...], b_ref[...])


def _combine(dest3, gates, x1, g, b, ys, dn_alpha):
    t = x1.shape[0]
    n = t // TOK_BLOCK
    tok = pl.BlockSpec((TOK_BLOCK, D_MODEL), lambda i: (i, 0))
    vec = pl.BlockSpec((1, D_MODEL), lambda i: (0, 0))
    return pl.pallas_call(
        functools.partial(_combine_kernel, dn_alpha=dn_alpha), grid=(n,),
        in_specs=[pl.BlockSpec((1, 1, ROWS_PER_STEP), lambda i: (i, 0, 0), memory_space=pltpu.SMEM),
                  pl.BlockSpec((1, 1, ROWS_PER_STEP), lambda i: (jnp.minimum(i + 1, n - 1), 0, 0),
                               memory_space=pltpu.SMEM),
                  pl.BlockSpec((TOK_BLOCK, LANE), lambda i: (i, 0)), tok, vec, vec,
                  pl.BlockSpec(memory_space=pl.ANY)],
        out_specs=tok,
        out_shape=jax.ShapeDtypeStruct((t, D_MODEL), F32),
        scratch_shapes=[pltpu.VMEM((2, TOP_K, TOK_BLOCK, D_MODEL), F32), pltpu.SemaphoreType.DMA((2,))],
        compiler_params=_cparams("arbitrary"), name="moe_combine_ln",
    )(dest3, dest3, gates, x1, g, b, ys)


def _onehots(idx):
    lane = lax.broadcasted_iota(jnp.int32, idx.shape, 1)
    return lane, [lane == idx[:, k:k + 1] for k in range(TOP_K)]


def _lanes(cols, lane):
    out = jnp.zeros(lane.shape, jnp.int32)
    for k, c in enumerate(cols):
        out = jnp.where(lane == k, c, out)
    return out


def _rank_kernel(cnt0_ref, idx_ref, rank_ref, cnt_ref, carry_ref):
    @pl.when(pl.program_id(0) == 0)
    def _():
        carry_ref[...] = cnt0_ref[...]

    n = TOK_BLOCK
    r = lax.broadcasted_iota(jnp.int32, (n, n), 0)
    c = lax.broadcasted_iota(jnp.int32, (n, n), 1)
    earlier = jnp.where(c < r, 1.0, 0.0).astype(BF16)
    carry = carry_ref[...]
    for sub in range(idx_ref.shape[0] // n):
        rows = slice(sub * n, (sub + 1) * n)
        lane, hots = _onehots(idx_ref[rows, :])
        chose = sum(h.astype(F32) for h in hots)
        before = _dot(earlier, chose.astype(BF16)) + carry
        ranks = [jnp.sum(jnp.where(h, before, 0.0), axis=-1, keepdims=True).astype(jnp.int32) for h in hots]
        rank_ref[rows, :] = _lanes(ranks, lane)
        carry = carry + jnp.sum(chose, axis=0, keepdims=True)
    carry_ref[...] = carry
    cnt_ref[...] = carry


def _route_block(t):
    return 1024 if t % 1024 == 0 else TOK_BLOCK


def _rank(cnt0, idx):
    t = idx.shape[0]
    tb = _route_block(t)
    tok = pl.BlockSpec((tb, LANE), lambda i: (i, 0))
    one = pl.BlockSpec((1, LANE), lambda i: (0, 0))
    return pl.pallas_call(
        _rank_kernel, grid=(t // tb,), in_specs=[one, tok], out_specs=[tok, one],
        out_shape=[jax.ShapeDtypeStruct((t, LANE), jnp.int32), jax.ShapeDtypeStruct((1, LANE), F32)],
        scratch_shapes=[pltpu.VMEM((1, LANE), F32)],
        compiler_params=_cparams("arbitrary"), name="moe_rank",
    )(cnt0, idx)


def _dest_kernel(cnt_ref, idx_ref, rank_ref, dest_ref, blk_ref, used_ref, start_ref):
    @pl.when(pl.program_id(0) == 0)
    def _():
        lane = lax.broadcasted_iota(jnp.int32, (8, LANE), 1)
        cnt = jnp.broadcast_to(cnt_ref[...], (8, LANE)).astype(jnp.int32)
        nblk = jnp.where(lane < N_EXPERTS, jnp.right_shift(cnt + (MOE_TM - 1), MOE_TM.bit_length() - 1), 0)
        end = nblk
        for s in (1, 2, 4, 8, 16):
            end = end + jnp.where(lane >= s, pltpu.roll(end, s, 1), 0)
        start_ref[...] = ((end - nblk) * MOE_TM).astype(F32)
        sub = lax.broadcasted_iota(jnp.int32, (8, LANE), 0)
        cnt = jnp.where(lane < N_EXPERTS, cnt, 0)
        used_ref[...] = jnp.where(sub == 0, jnp.broadcast_to(end[:, N_EXPERTS - 1:N_EXPERTS], (8, LANE)),
                                  jnp.where(sub == 1, (end - nblk) * MOE_TM + cnt,
                                            jnp.where(sub == 2, nblk * MOE_TM - cnt, 0)))
        blk = lax.broadcasted_iota(jnp.int32, blk_ref.shape, 0)
        lane_b = lax.broadcasted_iota(jnp.int32, blk_ref.shape, 1)
        done = jnp.where(lane_b < N_EXPERTS, jnp.where(end[0:1, :] <= blk, 1.0, 0.0), 0.0)
        blk_ref[...] = jnp.broadcast_to(
            jnp.minimum(jnp.sum(done, axis=-1, keepdims=True), N_EXPERTS - 1.0).astype(jnp.int32), blk_ref.shape)

    lane, hots = _onehots(idx_ref[...])
    start = start_ref[0:1, :]
    rank = rank_ref[...]
    dest_ref[...] = _lanes([jnp.sum(jnp.where(h, start, 0.0), axis=-1, keepdims=True).astype(jnp.int32)
                            + rank[:, k:k + 1] for k, h in enumerate(hots)], lane)


def _dest(cnt, idx, rank, n_blocks):
    t = idx.shape[0]
    tb = _route_block(t)
    tok = pl.BlockSpec((tb, LANE), lambda i: (i, 0))
    rows = -(-n_blocks // 8) * 8
    return pl.pallas_call(
        _dest_kernel, grid=(t // tb,),
        in_specs=[pl.BlockSpec((1, LANE), lambda i: (0, 0)), tok, tok],
        out_specs=[tok, pl.BlockSpec((rows, LANE), lambda i: (0, 0)), pl.BlockSpec((8, LANE), lambda i: (0, 0))],
        out_shape=[jax.ShapeDtypeStruct((t, LANE), jnp.int32), jax.ShapeDtypeStruct((rows, LANE), jnp.int32),
                   jax.ShapeDtypeStruct((8, LANE), jnp.int32)],
        scratch_shapes=[pltpu.VMEM((8, LANE), F32)],
        compiler_params=_cparams("arbitrary"), name="moe_dest",
    )(cnt, idx, rank)


def _moe(groups, w_gu, b_gu, w_dn, b_dn, g, b, dn_alpha):
    n_assign = sum(x1.shape[0] for x1, _, _ in groups) * TOP_K
    n_blocks = n_assign // MOE_TM + N_EXPERTS
    cnt = jnp.zeros((1, LANE), F32)
    ranks = []
    for _, idx, _ in groups:
        rank, cnt = _rank(cnt, idx)
        ranks.append(rank)
    dests = []
    for (x1, idx, _), rank in zip(groups, ranks):
        dest, blk, used = _dest(cnt, idx, rank, n_blocks)
        dests.append(dest[:, :TOP_K].reshape(-1, 1, ROWS_PER_STEP))
    (xa, _, _), (xb, _, _) = groups
    xs = _scatter_rows(used[1, :N_EXPERTS], used[2, :N_EXPERTS], used[0, :1], jnp.concatenate(dests, axis=0),
                       xa, xb, n_blocks)
    ys = _experts(blk[:n_blocks, 0], used[0, :1], xs, w_gu, b_gu, w_dn, b_dn)
    return [_combine(dest3, gates, x1, g, b, ys, dn_alpha) for (x1, _, gates), dest3 in zip(groups, dests)]


def _project_all(x, w_in, pos, tm_plain, tm_rope, keys_t=None, operand_dtype=F32):
    n_groups = COL // LANE
    ret_rope = _rope_table(pos, R_DK) + (R_DK, (1.0,) * R_HEADS + (R_DK ** -0.5,) * R_HEADS)
    diff_tab = _rope_table(pos, D_DH)
    diff_rope = diff_tab + (D_DH, (1.0,) * n_groups)
    qk = _proj(x, w_in, 0, COL, tm_rope, ret_rope)
    vr = _proj(x, w_in, 1 * COL, COL, tm_plain, out_dtype=operand_dtype)
    gr = _proj(x, w_in, 2 * COL, COL, tm_plain)
    qd = _proj(x, w_in, 3 * COL, COL, tm_rope, diff_rope, out_dtype=operand_dtype)
    if keys_t is None:
        kd = _proj(x, w_in, 4 * COL, COL, tm_rope, diff_rope)
    else:
        cos_t, sin_t = (t[:, :D_DH].T for t in diff_tab)
        kd = _proj_keys_t(x, w_in, 4 * COL, tm_rope, cos_t, sin_t, *keys_t)
    vd = _proj(x, w_in, 5 * COL, COL, tm_plain)
    ga = _proj(x, w_in, 6 * COL, COL, tm_plain)
    gb = _proj(x, w_in, 7 * COL, COL, tm_plain)
    return qk, vr, gr, qd, kd, vd, ga, gb


def kernel(x_prompt, x_sample, cache_k, cache_v, state_ret, page_table, w_in, w_branch_ret, w_branch_diff, w_out, lam_q1, lam_k1, lam_q2, lam_k2, subln_w, ln1_g, ln1_b, w_router, b_router, w_gate_up, b_gate_up, w_down, b_down, ln2_g, ln2_b):
    depth = w_in.shape[0]
    assert depth == 1, "single-layer trunk"
    batch, seq, _ = x_prompt.shape
    dbatch, dseq, _ = x_sample.shape
    dn_alpha = (2.0 * depth) ** 0.25
    lam_init = 0.8 - 0.6 * math.exp(-0.3 * 0)
    n_p, n_s = batch * seq, dbatch * dseq

    w_in_b = w_in[0].astype(BF16)
    wr_b, wd_b, wo_b = w_branch_ret[0].astype(BF16), w_branch_diff[0].astype(BF16), w_out[0].astype(BF16)
    w_rt_b = w_router[0].astype(BF16)
    lam_args = (lam_q1, lam_k1, lam_q2, lam_k2, subln_w)

    xp = x_prompt.reshape(n_p, D_MODEL)
    qk, vr, gr, qd, kd_t, vd, ga, gb = _project_all(xp.astype(BF16), w_in_b, jnp.arange(seq), 1024, 512,
                                                    (batch, seq), BF16)
    o_r, s_p = _retention_prompt(qk, vr, gr, batch, seq)
    o_d = _dattn_prompt(lam_args, qd, kd_t, vd, batch, seq, lam_init)
    x1_p, idx_p, gate_p = _mix(xp, o_r, o_d, ga, gb, wr_b, wd_b, wo_b, ln1_g, ln1_b, w_rt_b, b_router, 512, dn_alpha)

    xs_ = x_sample.reshape(n_s, D_MODEL)
    pos_s = jnp.tile(PAST_LEN + jnp.arange(dseq), dbatch)
    qk_s, vr_s, gr_s, qd_s, kd_s, vd_s, ga_s, gb_s = _project_all(xs_, w_in_b, pos_s, n_s, n_s)
    pad8 = lambda a: jnp.pad(a.reshape(dbatch, dseq, -1), ((0, 0), (0, 8 - dseq), (0, 0)))
    pad_page = lambda a: jnp.pad(a.reshape(dbatch, dseq, -1), ((0, 0), (0, PAGE_SIZE - dseq), (0, 0)))
    o_r_s, s_s = _retention_sample(pad8(qk_s), pad8(vr_s), pad8(gr_s), state_ret[0], dseq)
    n_pool = cache_k.shape[1]
    cache_kt = jnp.transpose(cache_k[0], (0, 2, 3, 4, 1)).reshape(n_pool, D_HEADS * 2 * D_DH, PAGE_SIZE)
    cache_vr = cache_v[0].reshape(n_pool, PAGE_SIZE * D_HEADS, D_DV)
    kn_t = jnp.transpose(pad_page(kd_s), (0, 2, 1))
    vn = pad_page(vd_s).reshape(dbatch, PAGE_SIZE * D_HEADS, D_DV)
    o_d_s = _dattn_decode(lam_args, page_table, pad8(qd_s), kn_t, vn, cache_kt, cache_vr, lam_init)
    o_r_s = o_r_s[:, :dseq].reshape(n_s, -1)
    o_d_s = o_d_s[:, :dseq].reshape(n_s, -1)
    x1_s, idx_s, gate_s = _mix(xs_, o_r_s, o_d_s, ga_s, gb_s, wr_b, wd_b, wo_b, ln1_g, ln1_b, w_rt_b, b_router,
                               n_s, dn_alpha)

    y_p, y_s = _moe([(x1_p, idx_p, gate_p), (x1_s, idx_s, gate_s)],
                    w_gate_up[0], b_gate_up[0][:, None, :], w_down[0], b_down[0][:, None, :], ln2_g, ln2_b, dn_alpha)

    return (y_p.reshape(batch, seq, D_MODEL),
            y_s.reshape(dbatch, dseq, D_MODEL),
            jnp.transpose(kd_t.reshape(batch, D_HEADS, 2, D_DH, seq), (0, 4, 1, 2, 3))[None],
            vd.reshape(1, batch, seq, D_HEADS, D_DV),
            s_p[None],
            kd_s.reshape(1, dbatch, dseq, D_HEADS, 2, D_DH),
            vd_s.reshape(1, dbatch, dseq, D_HEADS, D_DV),
            s_s[None])
```
